```python
import math
import jax, jax.numpy as jnp
from jax import lax
import numpy as np

D_MODEL = 4096
BATCH = 4
SEQ = 2048
DEPTH = 2

D_MIX = D_MODEL
HEAD_DIM = 128
ATTN_WIDTH = D_MIX // 2
N_Q_HEADS = ATTN_WIDTH // HEAD_DIM
N_KV_HEADS = N_Q_HEADS // 4
Q_PER_KV = N_Q_HEADS // N_KV_HEADS
KV_WIDTH = N_KV_HEADS * HEAD_DIM
RG_WIDTH = D_MIX // 4
RG_BLOCK = 128
N_RG_BLOCKS = RG_WIDTH // RG_BLOCK
RG_CONV_W = 4
RG_C = 8.0
CV_WIDTH = D_MIX - ATTN_WIDTH - RG_WIDTH
CV_KERNEL = 31
D_IN = ATTN_WIDTH + 2 * KV_WIDTH + 2 * RG_WIDTH + 2 * CV_WIDTH
OFF_Q = 0
OFF_K = OFF_Q + ATTN_WIDTH
OFF_V = OFF_K + KV_WIDTH
OFF_RX = OFF_V + KV_WIDTH
OFF_RG = OFF_RX + RG_WIDTH
OFF_CV = OFF_RG + RG_WIDTH
D_FF = -(-8 * D_MODEL // (3 * 256)) * 256
N_MEM = 256
MEM_HEADS = 4
MEM_HEAD_DIM = D_MODEL // MEM_HEADS
GRID_W = 64
Q_BLOCK = 128
ROPE_THETA = 10000.0
ROPE_AXIS_DIM = HEAD_DIM // 2
EPS = 1e-6

kernel_name = "hymba_style_bidir_hybrid_encoder"


def rmsnorm(x, g):
    xf = x.astype(jnp.float32)
    y = xf * lax.rsqrt(jnp.mean(xf * xf, axis=-1, keepdims=True) + EPS) * g.astype(jnp.float32)
    return y.astype(x.dtype)


def layernorm(x, g, b):
    xf = x.astype(jnp.float32)
    mu = jnp.mean(xf, axis=-1, keepdims=True)
    xc = xf - mu
    var = jnp.mean(xc * xc, axis=-1, keepdims=True)
    y = xc * lax.rsqrt(var + EPS) * g.astype(jnp.float32) + b.astype(jnp.float32)
    return y.astype(x.dtype)


def dwconv(x, w, b, pad):
    c = x.shape[-1]
    y = lax.conv_general_dilated(
        x, w.astype(x.dtype)[:, None, :], window_strides=(1,), padding=[pad],
        dimension_numbers=("NWC", "WIO", "NWC"), feature_group_count=c)
    return y + b.astype(x.dtype)


def axial_rope_tables(seq_len):
    rows = seq_len // GRID_W
    row = jnp.repeat(jnp.arange(rows), GRID_W).astype(jnp.float32)
    col = jnp.tile(jnp.arange(GRID_W), rows).astype(jnp.float32)
    inv = ROPE_THETA ** (-jnp.arange(0, ROPE_AXIS_DIM, 2, dtype=jnp.float32) / ROPE_AXIS_DIM)
    ang = jnp.stack([row[:, None] * inv, col[:, None] * inv], axis=1)
    return jnp.cos(ang), jnp.sin(ang)


def apply_axial_rope(x, cos, sin):
    b, s, h, _ = x.shape
    xr = x.reshape(b, s, h, 2, 2, ROPE_AXIS_DIM // 2)
    x1, x2 = xr[..., 0, :], xr[..., 1, :]
    c = cos[None, :, None]
    sn = sin[None, :, None]
    out = jnp.stack([x1 * c - x2 * sn, x2 * c + x1 * sn], axis=-2)
    return out.reshape(b, s, h, HEAD_DIM)


def attention_group(q, k, v, g_q, g_k, cos, sin):
    b, s, _ = q.shape
    q = q.reshape(b, s, N_Q_HEADS, HEAD_DIM)
    k = k.reshape(b, s, N_KV_HEADS, HEAD_DIM)
    v = v.reshape(b, s, N_KV_HEADS, HEAD_DIM)
    q = apply_axial_rope(rmsnorm(q, g_q).astype(jnp.float32), cos, sin).astype(v.dtype)
    k = apply_axial_rope(rmsnorm(k, g_k).astype(jnp.float32), cos, sin).astype(v.dtype)
    scale = HEAD_DIM ** -0.5
    nb = s // Q_BLOCK
    qb = q.reshape(b, nb, Q_BLOCK, N_KV_HEADS, Q_PER_KV, HEAD_DIM).transpose(1, 0, 2, 3, 4, 5)

    def block(qi):
        sc = jnp.einsum("bqkgd,bskd->bkgqs", qi, k).astype(jnp.float32) * scale
        p = jax.nn.softmax(sc, axis=-1).astype(v.dtype)
        return jnp.einsum("bkgqs,bskd->bqkgd", p, v)

    o = lax.map(block, qb)
    return o.transpose(1, 0, 2, 3, 4, 5).reshape(b, s, ATTN_WIDTH)


def linear_scan(a, u, reverse):
    a_tm = jnp.swapaxes(a, 0, 1)
    u_tm = jnp.swapaxes(u, 0, 1)

    def step(h, au):
        a_t, u_t = au
        h = a_t * h + u_t
        return h, h

    h0 = jnp.zeros(a_tm.shape[1:], jnp.float32)
    _, hs = lax.scan(step, h0, (a_tm, u_tm), reverse=reverse)
    return jnp.swapaxes(hs, 0, 1)


def rglru_direction(xh, xf, w_a, b_a, w_x, b_x, lam, reverse):
    b, s = xf.shape[:2]
    r = jax.nn.sigmoid((jnp.einsum("bshi,hij->bshj", xh, w_a.astype(xh.dtype)).reshape(b, s, RG_WIDTH)
                        + b_a.astype(xh.dtype)).astype(jnp.float32))
    i = jax.nn.sigmoid((jnp.einsum("bshi,hij->bshj", xh, w_x.astype(xh.dtype)).reshape(b, s, RG_WIDTH)
                        + b_x.astype(xh.dtype)).astype(jnp.float32))
    log_a = -RG_C * r * jax.nn.softplus(-lam.astype(jnp.float32))
    a = jnp.exp(log_a)
    u = jnp.sqrt(-jnp.expm1(2.0 * log_a)) * (i * xf)
    return linear_scan(a, u, reverse)


def rglru_group(xb, gate, conv_w, conv_b, w_a, b_a, w_x, b_x, lam):
    b, s, _ = xb.shape
    left = RG_CONV_W // 2
    xc = dwconv(xb, conv_w, conv_b, (left, RG_CONV_W - 1 - left))
    xh = xc.reshape(b, s, N_RG_BLOCKS, RG_BLOCK)
    xf = xc.astype(jnp.float32)
    h_fwd = rglru_direction(xh, xf, w_a[0], b_a[0], w_x[0], b_x[0], lam[0], False)
    h_bwd = rglru_direction(xh, xf, w_a[1], b_a[1], w_x[1], b_x[1], lam[1], True)
    y = (h_fwd + h_bwd) * jax.nn.gelu(gate.astype(jnp.float32))
    return y.astype(xb.dtype)


def conformer_conv_group(c, dw_w, dw_b, ln_g, ln_b, pw_w, pw_b):
    glu = c[..., :CV_WIDTH] * jax.nn.sigmoid(c[..., CV_WIDTH:])
    y = dwconv(glu, dw_w, dw_b, (CV_KERNEL // 2, CV_KERNEL // 2))
    y = jax.nn.silu(layernorm(y, ln_g, ln_b))
    return y @ pw_w + pw_b


def memory_cross_attention(h, m, wq, wk, wv, wo):
    b, s, _ = h.shape
    n = m.shape[1]
    q = (h @ wq).reshape(b, s, MEM_HEADS, MEM_HEAD_DIM)
    k = (m @ wk).reshape(b, n, MEM_HEADS, MEM_HEAD_DIM)
    v = (m @ wv).reshape(b, n, MEM_HEADS, MEM_HEAD_DIM)
    sc = jnp.einsum("bshd,bmhd->bhsm", q, k).astype(jnp.float32) * (MEM_HEAD_DIM ** -0.5)
    p = jax.nn.softmax(sc, axis=-1).astype(v.dtype)
    o = jnp.einsum("bhsm,bmhd->bshd", p, v).reshape(b, s, D_MODEL)
    return o @ wo


def setup_inputs(seed: int = 0) -> dict:
    key = jax.random.key(seed)
    ks = iter(jax.random.split(key, 40))
    f32 = jnp.float32

    def nrm(shape, scale):
        return jax.random.normal(next(ks), shape, f32) * scale

    def gain(shape):
        return 1.0 + nrm(shape, 0.02)

    u = jax.random.uniform(next(ks), (DEPTH, 2, RG_WIDTH), f32, minval=0.9, maxval=0.999)
    a0 = u ** (1.0 / RG_C)
    rg_lam = jnp.log(a0) - jnp.log1p(-a0)

    return {
        "x": nrm((BATCH, SEQ, D_MODEL), 1.0),
        "mem": nrm((BATCH, N_MEM, D_MODEL), 1.0),
        "g_mix": gain((DEPTH, D_MODEL)),
        "w_in": nrm((DEPTH, D_MODEL, D_IN), D_MODEL ** -0.5),
        "g_q": gain((DEPTH, HEAD_DIM)),
        "g_k": gain((DEPTH, HEAD_DIM)),
        "rg_conv_w": nrm((DEPTH, RG_CONV_W, RG_WIDTH), RG_CONV_W ** -0.5),
        "rg_conv_b": nrm((DEPTH, RG_WIDTH), 0.02),
        "rg_w_a": nrm((DEPTH, 2, N_RG_BLOCKS, RG_BLOCK, RG_BLOCK), RG_BLOCK ** -0.5),
        "rg_b_a": nrm((DEPTH, 2, RG_WIDTH), 0.02),
        "rg_w_x": nrm((DEPTH, 2, N_RG_BLOCKS, RG_BLOCK, RG_BLOCK), RG_BLOCK ** -0.5),
        "rg_b_x": nrm((DEPTH, 2, RG_WIDTH), 0.02),
        "rg_lam": rg_lam,
        "cv_dw_w": nrm((DEPTH, CV_KERNEL, CV_WIDTH), CV_KERNEL ** -0.5),
        "cv_dw_b": nrm((DEPTH, CV_WIDTH), 0.02),
        "cv_ln_g": gain((DEPTH, CV_WIDTH)),
        "cv_ln_b": nrm((DEPTH, CV_WIDTH), 0.02),
        "cv_pw_w": nrm((DEPTH, CV_WIDTH, CV_WIDTH), CV_WIDTH ** -0.5),
        "cv_pw_b": nrm((DEPTH, CV_WIDTH), 0.02),
        "g_grp": gain((DEPTH, D_MIX)),
        "w_out": nrm((DEPTH, D_MIX, D_MODEL), D_MIX ** -0.5),
        "g_xattn": gain((DEPTH, D_MODEL)),
        "g_mem": gain((DEPTH, D_MODEL)),
        "xa_wq": nrm((DEPTH, D_MODEL, D_MODEL), D_MODEL ** -0.5),
        "xa_wk": nrm((DEPTH, D_MODEL, D_MODEL), D_MODEL ** -0.5),
        "xa_wv": nrm((DEPTH, D_MODEL, D_MODEL), D_MODEL ** -0.5),
        "xa_wo": nrm((DEPTH, D_MODEL, D_MODEL), D_MODEL ** -0.5),
        "g_ffn": gain((DEPTH, D_MODEL)),
        "ffn_wg": nrm((DEPTH, D_MODEL, D_FF), D_MODEL ** -0.5),
        "ffn_wu": nrm((DEPTH, D_MODEL, D_FF), D_MODEL ** -0.5),
        "ffn_wd": nrm((DEPTH, D_FF, D_MODEL), D_FF ** -0.5),
        "g_final": gain((D_MODEL,)),
    }


def reference(x, mem, g_mix, w_in, g_q, g_k, rg_conv_w, rg_conv_b, rg_w_a, rg_b_a, rg_w_x, rg_b_x,
              rg_lam, cv_dw_w, cv_dw_b, cv_ln_g, cv_ln_b, cv_pw_w, cv_pw_b, g_grp, w_out,
              g_xattn, g_mem, xa_wq, xa_wk, xa_wv, xa_wo, g_ffn, ffn_wg, ffn_wu, ffn_wd, g_final):
    seq_len = x.shape[1]
    cos, sin = axial_rope_tables(seq_len)

    for l in range(DEPTH):
        h = rmsnorm(x, g_mix[l])
        z = h @ w_in[l]
        y_attn = attention_group(z[..., OFF_Q:OFF_K], z[..., OFF_K:OFF_V], z[..., OFF_V:OFF_RX],
                                 g_q[l], g_k[l], cos, sin)
        y_rec = rglru_group(z[..., OFF_RX:OFF_RG], z[..., OFF_RG:OFF_CV], rg_conv_w[l], rg_conv_b[l],
                            rg_w_a[l], rg_b_a[l], rg_w_x[l], rg_b_x[l], rg_lam[l])
        y_cv = conformer_conv_group(z[..., OFF_CV:], cv_dw_w[l], cv_dw_b[l], cv_ln_g[l], cv_ln_b[l],
                                    cv_pw_w[l], cv_pw_b[l])
        gg = g_grp[l]
        y = jnp.concatenate([
            rmsnorm(y_attn, gg[:ATTN_WIDTH]),
            rmsnorm(y_rec, gg[ATTN_WIDTH:ATTN_WIDTH + RG_WIDTH]),
            rmsnorm(y_cv, gg[ATTN_WIDTH + RG_WIDTH:]),
        ], axis=-1)
        x = x + y @ w_out[l]

        x = x + memory_cross_attention(rmsnorm(x, g_xattn[l]), rmsnorm(mem, g_mem[l]),
                                       xa_wq[l], xa_wk[l], xa_wv[l], xa_wo[l])

        h = rmsnorm(x, g_ffn[l])
        x = x + (jax.nn.silu(h @ ffn_wg[l]) * (h @ ffn_wu[l])) @ ffn_wd[l]

    return rmsnorm(x, g_final)
```

```python
import functools
import math

import jax
import jax.numpy as jnp
from jax import lax
from jax.experimental import pallas as pl
from jax.experimental.pallas import tpu as pltpu

F32 = jnp.float32
BF16 = jnp.bfloat16

HEAD_DIM = 128
N_Q_HEADS = 16
N_KV_HEADS = 4
Q_PER_KV = N_Q_HEADS // N_KV_HEADS
ATTN_WIDTH = N_Q_HEADS * HEAD_DIM
KV_WIDTH = N_KV_HEADS * HEAD_DIM
RG_WIDTH = 1024
RG_BLOCK = 128
RG_CONV_W = 4
RG_C = 8.0
CV_WIDTH = 1024
CV_KERNEL = 31
QKV_WIDTH = ATTN_WIDTH + 2 * KV_WIDTH
COL_RX = QKV_WIDTH // RG_WIDTH
COL_RG = COL_RX + 1
COL_CA = COL_RG + 1
COL_CB = COL_CA + 1
MEM_HEADS = 4
GRID_W = 64
ROPE_THETA = 10000.0
ROPE_AXIS_DIM = HEAD_DIM // 2
EPS = 1e-6

V7X_VMEM_BYTES = 64 * 1024 * 1024
V7X_SUBLANES = 8
V7X_LANES = 128
VMEM_LIMIT = V7X_VMEM_BYTES - 8 * 1024 * 1024

CV_HALO = 16


def _cparams(sem):
    return pltpu.CompilerParams(dimension_semantics=sem, vmem_limit_bytes=VMEM_LIMIT)


def _rmsnorm_kernel(x_ref, g_ref, o_ref):
    x = x_ref[...].astype(F32)
    ms = jnp.mean(x * x, axis=-1, keepdims=True)
    o_ref[...] = (x * lax.rsqrt(ms + EPS) * g_ref[...]).astype(o_ref.dtype)


def rmsnorm(x, g3, layer, gcol, out_dtype, name, tm=256):
    m, w = x.shape
    tm = min(tm, m)
    return pl.pallas_call(
        _rmsnorm_kernel,
        out_shape=jax.ShapeDtypeStruct((m, w), out_dtype),
        grid=(m // tm,),
        in_specs=[
            pl.BlockSpec((tm, w), lambda i: (i, 0)),
            pl.BlockSpec((None, 1, w), lambda i: (layer, 0, gcol)),
        ],
        out_specs=pl.BlockSpec((tm, w), lambda i: (i, 0)),
        compiler_params=_cparams(("arbitrary",)),
        name=name,
    )(x, g3)


CAST_ROWS = 256


def _cast_panel(w_ref, wb_ref):
    k = w_ref.shape[0]

    def body(i, c):
        r = pl.multiple_of(i * CAST_ROWS, CAST_ROWS)
        wb_ref[pl.ds(r, CAST_ROWS), :] = w_ref[pl.ds(r, CAST_ROWS), :].astype(BF16)
        return c

    lax.fori_loop(0, k // CAST_ROWS, body, 0)


def _mm_kernel(*refs, k_splits, has_bias, has_res, w_is_bf16):
    n_a = len(k_splits)
    a_refs = refs[:n_a]
    w_ref = refs[n_a]
    pos = n_a + 1
    bias_ref = res_ref = None
    if has_bias:
        bias_ref = refs[pos]
        pos += 1
    if has_res:
        res_ref = refs[pos]
        pos += 1
    o_ref = refs[pos]
    if w_is_bf16:
        wb_ref = w_ref
    else:
        wb_ref = refs[pos + 1]

        @pl.when(pl.program_id(1) == 0)
        def _():
            _cast_panel(w_ref, wb_ref)

    acc = None
    off = 0
    for a_ref, kk in zip(a_refs, k_splits):
        d = jnp.dot(a_ref[...], wb_ref[off:off + kk, :], preferred_element_type=F32)
        acc = d if acc is None else acc + d
        off += kk
    if has_bias:
        acc = acc + bias_ref[...]
    if has_res:
        acc = acc + res_ref[...]
    o_ref[...] = acc.astype(o_ref.dtype)


def matmul(a_list, w3, layer, out_dtype, name, *, tn, tm, bias3=None, res=None):
    m = a_list[0].shape[0]
    k_splits = tuple(a.shape[1] for a in a_list)
    _, k, n = w3.shape
    assert sum(k_splits) == k
    tm = min(tm, m)
    w_is_bf16 = w3.dtype == BF16
    in_specs = [pl.BlockSpec((tm, kk), lambda j, i: (i, 0)) for kk in k_splits]
    in_specs.append(pl.BlockSpec((None, k, tn), lambda j, i: (layer, 0, j)))
    args = list(a_list) + [w3]
    if bias3 is not None:
        in_specs.append(pl.BlockSpec((None, 1, tn), lambda j, i: (layer, 0, j)))
        args.append(bias3)
    if res is not None:
        in_specs.append(pl.BlockSpec((tm, tn), lambda j, i: (i, j)))
        args.append(res)
    scratch = [] if w_is_bf16 else [pltpu.VMEM((k, tn), BF16)]
    kern = functools.partial(_mm_kernel, k_splits=k_splits, has_bias=bias3 is not None,
                             has_res=res is not None, w_is_bf16=w_is_bf16)
    return pl.pallas_call(
        kern,
        out_shape=jax.ShapeDtypeStruct((m, n), out_dtype),
        grid=(n // tn, m // tm),
        in_specs=in_specs,
        out_specs=pl.BlockSpec((tm, tn), lambda j, i: (i, j)),
        scratch_shapes=scratch,
        compiler_params=_cparams(("arbitrary", "arbitrary")),
        name=name,
    )(*args)


def _swiglu_kernel(a_ref, wg0_ref, wg1_ref, wu0_ref, wu1_ref, o_ref, wgb_ref, wub_ref):
    half = wg0_ref.shape[1]

    @pl.when(pl.program_id(1) == 0)
    def _():
        _cast_panel(wg0_ref, wgb_ref.at[:, 0:half])
        _cast_panel(wg1_ref, wgb_ref.at[:, half:2 * half])
        _cast_panel(wu0_ref, wub_ref.at[:, 0:half])
        _cast_panel(wu1_ref, wub_ref.at[:, half:2 * half])

    a = a_ref[...]
    g = jnp.dot(a, wgb_ref[...], preferred_element_type=F32)
    u = jnp.dot(a, wub_ref[...], preferred_element_type=F32)
    o_ref[...] = (g * jax.nn.sigmoid(g) * u).astype(o_ref.dtype)


def swiglu(a, wg3, wu3, layer, name, *, tn, tm):
    m, k = a.shape
    n = wg3.shape[2]
    tm = min(tm, m)
    half = tn // 2
    n_half = n // half
    assert n % half == 0
    w0 = pl.BlockSpec((None, k, half), lambda j, i: (layer, 0, 2 * j))
    w1 = pl.BlockSpec((None, k, half), lambda j, i: (layer, 0, jnp.minimum(2 * j + 1, n_half - 1)))
    return pl.pallas_call(
        _swiglu_kernel,
        out_shape=jax.ShapeDtypeStruct((m, n), BF16),
        grid=(pl.cdiv(n, tn), m // tm),
        in_specs=[pl.BlockSpec((tm, k), lambda j, i: (i, 0)), w0, w1, w0, w1],
        out_specs=pl.BlockSpec((tm, tn), lambda j, i: (i, j)),
        scratch_shapes=[pltpu.VMEM((k, tn), BF16), pltpu.VMEM((k, tn), BF16)],
        compiler_params=_cparams(("arbitrary", "arbitrary")),
        name=name,
    )(a, wg3, wg3, wu3, wu3)


def _cast_kernel(x_ref, o_ref):
    o_ref[...] = x_ref[...].astype(o_ref.dtype)


def cast_bf16(w3, name, rows=256):
    l, k, n = w3.shape
    return pl.pallas_call(
        _cast_kernel,
        out_shape=jax.ShapeDtypeStruct((l, k, n), BF16),
        grid=(l, k // rows),
        in_specs=[pl.BlockSpec((None, rows, n), lambda a, i: (a, i, 0))],
        out_specs=pl.BlockSpec((None, rows, n), lambda a, i: (a, i, 0)),
        compiler_params=_cparams(("arbitrary", "arbitrary")),
        name=name,
    )(w3)


def _qkv_prep_kernel(z_ref, gq_ref, gk_ref, cos_ref, sin_ref, q_ref, k_ref, v_ref):
    cos = cos_ref[...]
    sin = sin_ref[...]
    lane = lax.broadcasted_iota(jnp.int32, cos.shape, 1)
    first_half = (lane % ROPE_AXIS_DIM) < (ROPE_AXIS_DIM // 2)
    half = ROPE_AXIS_DIM // 2

    def norm_rope(x, g):
        ms = jnp.mean(x * x, axis=-1, keepdims=True)
        xn = x * lax.rsqrt(ms + EPS) * g
        swapped = jnp.where(first_half, pltpu.roll(xn, HEAD_DIM - half, axis=1), pltpu.roll(xn, half, axis=1))
        return xn * cos + swapped * sin

    gq = gq_ref[...]
    gk = gk_ref[...]
    for h in range(N_Q_HEADS):
        c0 = h * HEAD_DIM
        q_ref[:, c0:c0 + HEAD_DIM] = norm_rope(z_ref[:, c0:c0 + HEAD_DIM], gq).astype(q_ref.dtype)
    for h in range(N_KV_HEADS):
        c0 = h * HEAD_DIM
        zc = ATTN_WIDTH + c0
        k_ref[:, c0:c0 + HEAD_DIM] = norm_rope(z_ref[:, zc:zc + HEAD_DIM], gk).astype(k_ref.dtype)
    v_ref[...] = z_ref[:, ATTN_WIDTH + KV_WIDTH:QKV_WIDTH].astype(v_ref.dtype)


def qkv_prep(z, gq3, gk3, cos_t, sin_t, layer, seq, name, tm=256):
    m = z.shape[0]
    tm = min(tm, seq)
    nsb = seq // tm
    tspec = pl.BlockSpec((tm, HEAD_DIM), lambda i: (i % nsb, 0))
    gspec = pl.BlockSpec((None, 1, HEAD_DIM), lambda i: (layer, 0, 0))
    return pl.pallas_call(
        _qkv_prep_kernel,
        out_shape=(jax.ShapeDtypeStruct((m, ATTN_WIDTH), BF16),
                   jax.ShapeDtypeStruct((m, KV_WIDTH), BF16),
                   jax.ShapeDtypeStruct((m, KV_WIDTH), BF16)),
        grid=(m // tm,),
        in_specs=[pl.BlockSpec((tm, QKV_WIDTH), lambda i: (i, 0)), gspec, gspec, tspec, tspec],
        out_specs=(pl.BlockSpec((tm, ATTN_WIDTH), lambda i: (i, 0)),
                   pl.BlockSpec((tm, KV_WIDTH), lambda i: (i, 0)),
                   pl.BlockSpec((tm, KV_WIDTH), lambda i: (i, 0))),
        compiler_params=_cparams(("arbitrary",)),
        name=name,
    )(z, gq3, gk3, cos_t, sin_t)


def rope_tables(seq):
    rows = seq // GRID_W
    row = jnp.repeat(jnp.arange(rows), GRID_W).astype(F32)
    col = jnp.tile(jnp.arange(GRID_W), rows).astype(F32)
    inv = ROPE_THETA ** (-jnp.arange(0, ROPE_AXIS_DIM, 2, dtype=F32) / ROPE_AXIS_DIM)
    ang_r = row[:, None] * inv
    ang_c = col[:, None] * inv
    cos_t = jnp.concatenate([jnp.cos(ang_r), jnp.cos(ang_r), jnp.cos(ang_c), jnp.cos(ang_c)], axis=1)
    sin_t = jnp.concatenate([-jnp.sin(ang_r), jnp.sin(ang_r), -jnp.sin(ang_c), jnp.sin(ang_c)], axis=1)
    return cos_t, sin_t


def _attn_kernel(q_ref, k_ref, v_ref, o_ref):
    k = k_ref[...]
    v = v_ref[...]
    scale = HEAD_DIM ** -0.5
    for h in range(Q_PER_KV):
        c0 = h * HEAD_DIM
        q = q_ref[:, c0:c0 + HEAD_DIM]
        s = lax.dot_general(q, k, (((1,), (1,)), ((), ())), preferred_element_type=F32) * scale
        mx = jnp.max(s, axis=-1, keepdims=True)
        p = jnp.exp(s - mx)
        den = jnp.sum(p, axis=-1, keepdims=True)
        o = jnp.dot(p.astype(v.dtype), v, preferred_element_type=F32)
        o_ref[:, c0:c0 + HEAD_DIM] = (o / den).astype(o_ref.dtype)


def attention(q, k, v, batch, seq, name, tq=256):
    m = q.shape[0]
    tq = min(tq, seq)
    nq = seq // tq
    gw = Q_PER_KV * HEAD_DIM
    return pl.pallas_call(
        _attn_kernel,
        out_shape=jax.ShapeDtypeStruct((m, ATTN_WIDTH), F32),
        grid=(batch, N_KV_HEADS, nq),
        in_specs=[
            pl.BlockSpec((tq, gw), lambda b, g, i: (b * nq + i, g)),
            pl.BlockSpec((seq, HEAD_DIM), lambda b, g, i: (b, g)),
            pl.BlockSpec((seq, HEAD_DIM), lambda b, g, i: (b, g)),
        ],
        out_specs=pl.BlockSpec((tq, gw), lambda b, g, i: (b * nq + i, g)),
        compiler_params=_cparams(("arbitrary", "arbitrary", "arbitrary")),
        name=name,
    )(q, k, v)


def _xattn_kernel(q_ref, k_ref, v_ref, o_ref, *, head_dim):
    scale = head_dim ** -0.5
    for h in range(MEM_HEADS):
        c0 = h * head_dim
        q = q_ref[:, c0:c0 + head_dim]
        k = k_ref[:, c0:c0 + head_dim]
        v = v_ref[:, c0:c0 + head_dim]
        s = lax.dot_general(q, k, (((1,), (1,)), ((), ())), preferred_element_type=F32) * scale
        mx = jnp.max(s, axis=-1, keepdims=True)
        p = jnp.exp(s - mx)
        den = jnp.sum(p, axis=-1, keepdims=True)
        o = jnp.dot(p.astype(v.dtype), v, preferred_element_type=F32)
        o_ref[:, c0:c0 + head_dim] = (o / den).astype(o_ref.dtype)


def cross_attention(q, k, v, batch, seq, n_mem, name, tq=512):
    m, d = q.shape
    tq = min(tq, seq)
    nq = seq // tq
    kern = functools.partial(_xattn_kernel, head_dim=d // MEM_HEADS)
    return pl.pallas_call(
        kern,
        out_shape=jax.ShapeDtypeStruct((m, d), BF16),
        grid=(batch, nq),
        in_specs=[
            pl.BlockSpec((tq, d), lambda b, i: (b * nq + i, 0)),
            pl.BlockSpec((n_mem, d), lambda b, i: (b, 0)),
            pl.BlockSpec((n_mem, d), lambda b, i: (b, 0)),
        ],
        out_specs=pl.BlockSpec((tq, d), lambda b, i: (b * nq + i, 0)),
        compiler_params=_cparams(("arbitrary", "arbitrary")),
        name=name,
    )(q, k, v)


RG_PAD = V7X_SUBLANES


def _rglru_kernel(x_ref, gate_ref, cw_ref, cb_ref, w_ref, ba_ref, bx_ref, lam_ref, y_ref,
                  pad_ref, af_ref, uf_ref, ab_ref, ub_ref, *, seq, heads):
    cw = heads * RG_BLOCK
    zeros = jnp.zeros((RG_PAD, cw), F32)
    pad_ref[0:RG_PAD, :] = zeros
    pad_ref[RG_PAD + seq:RG_PAD + seq + RG_PAD, :] = zeros
    pad_ref[RG_PAD:RG_PAD + seq, :] = x_ref[...]
    left = RG_CONV_W // 2
    xc = cb_ref[...] + cw_ref[0:1, :] * pad_ref[pl.ds(RG_PAD - left, seq), :]
    for j in range(1, RG_CONV_W):
        xc = xc + cw_ref[j:j + 1, :] * pad_ref[pl.ds(RG_PAD - left + j, seq), :]
    xcb = xc.astype(BF16)

    lam = lam_ref[...]
    nl = -lam
    softplus = jnp.maximum(nl, 0.0) + jnp.log1p(jnp.exp(-jnp.abs(nl)))
    ba = ba_ref[...]
    bx = bx_ref[...]
    for h in range(heads):
        c0 = h * RG_BLOCK
        sl = slice(c0, c0 + RG_BLOCK)
        gates = jnp.dot(xcb[:, sl], w_ref[h].astype(BF16), preferred_element_type=F32)
        xch = xc[:, sl]
        for d, (a_ref, u_ref) in enumerate(((af_ref, uf_ref), (ab_ref, ub_ref))):
            g0 = 2 * d * RG_BLOCK
            r = jax.nn.sigmoid(gates[:, g0:g0 + RG_BLOCK] + ba[d:d + 1, sl])
            i = jax.nn.sigmoid(gates[:, g0 + RG_BLOCK:g0 + 2 * RG_BLOCK] + bx[d:d + 1, sl])
            log_a = (-RG_C) * r * softplus[d:d + 1, sl]
            t = jnp.tanh(log_a)
            one_minus_a2 = (-2.0 * t) / (1.0 - t)
            a_ref[:, sl] = jnp.exp(log_a)
            u_ref[:, sl] = jnp.sqrt(one_minus_a2) * (i * xch)

    n_tiles = seq // V7X_SUBLANES
    row = lax.broadcasted_iota(jnp.int32, (V7X_SUBLANES, cw), 0)

    def tile_scan(a, u, reverse):
        for d in (1, 2, 4):
            if reverse:
                shift, keep = V7X_SUBLANES - d, row < V7X_SUBLANES - d
            else:
                shift, keep = d, row >= d
            a_s = pltpu.roll(a, shift, axis=0)
            u_s = pltpu.roll(u, shift, axis=0)
            u = jnp.where(keep, u + a * u_s, u)
            a = jnp.where(keep, a * a_s, a)
        return a, u

    def body(t, carry):
        hf, hb = carry
        rf = pl.multiple_of(t * V7X_SUBLANES, V7X_SUBLANES)
        a, u = tile_scan(af_ref[pl.ds(rf, V7X_SUBLANES), :], uf_ref[pl.ds(rf, V7X_SUBLANES), :], False)
        hh = a * hf + u
        uf_ref[pl.ds(rf, V7X_SUBLANES), :] = hh
        hf = jnp.broadcast_to(hh[V7X_SUBLANES - 1:V7X_SUBLANES, :], hh.shape)
        rb = pl.multiple_of((n_tiles - 1 - t) * V7X_SUBLANES, V7X_SUBLANES)
        a, u = tile_scan(ab_ref[pl.ds(rb, V7X_SUBLANES), :], ub_ref[pl.ds(rb, V7X_SUBLANES), :], True)
        hh = a * hb + u
        ub_ref[pl.ds(rb, V7X_SUBLANES), :] = hh
        hb = jnp.broadcast_to(hh[0:1, :], hh.shape)
        return hf, hb

    h0 = jnp.zeros((V7X_SUBLANES, cw), F32)
    lax.fori_loop(0, n_tiles, body, (h0, h0))

    g = gate_ref[...]
    gelu = 0.5 * g * (1.0 + jnp.tanh(math.sqrt(2.0 / math.pi) * (g + 0.044715 * (g * g * g))))
    y_ref[...] = ((uf_ref[...] + ub_ref[...]) * gelu).astype(y_ref.dtype)


def rglru(z3, cw3, cb3, w4, ba3, bx3, lam3, layer, name, heads=2):
    b, seq, _ = z3.shape
    cw = heads * RG_BLOCK
    ncb = RG_WIDTH // cw
    n_rg_heads = RG_WIDTH // RG_BLOCK
    kern = functools.partial(_rglru_kernel, seq=seq, heads=heads)
    vec = lambda rows: pl.BlockSpec((None, rows, cw), lambda bi, j: (layer, 0, j))
    return pl.pallas_call(
        kern,
        out_shape=jax.ShapeDtypeStruct((b, seq, RG_WIDTH), F32),
        grid=(b, ncb),
        in_specs=[
            pl.BlockSpec((None, seq, cw), lambda bi, j: (bi, 0, COL_RX * ncb + j)),
            pl.BlockSpec((None, seq, cw), lambda bi, j: (bi, 0, COL_RG * ncb + j)),
            vec(RG_CONV_W), vec(1),
            pl.BlockSpec((heads, RG_BLOCK, 4 * RG_BLOCK), lambda bi, j: ((layer * n_rg_heads) // heads + j, 0, 0)),
            vec(2), vec(2), vec(2),
        ],
        out_specs=pl.BlockSpec((None, seq, cw), lambda bi, j: (bi, 0, j)),
        scratch_shapes=[pltpu.VMEM((seq + 2 * RG_PAD, cw), F32)] + [pltpu.VMEM((seq, cw), F32)] * 4,
        compiler_params=_cparams(("arbitrary", "arbitrary")),
        name=name,
    )(z3, z3, cw3, cb3, w4, ba3, bx3, lam3)


CV_CHUNK = 32


def _conformer_kernel(a_ref, g_ref, ap_ref, gp_ref, an_ref, gn_ref, dw_ref, db_ref, lg_ref, lb_ref,
                      pw_ref, pb_ref, gg_ref, o_ref, buf_ref, sh_ref, conv_ref, pwb_ref, *, ts):
    i = pl.program_id(1)
    n_i = pl.num_programs(1)

    @pl.when(jnp.logical_and(pl.program_id(0) == 0, i == 0))
    def _():
        pwb_ref[...] = pw_ref[...].astype(BF16)

    def glu(a, g):
        return a * jax.nn.sigmoid(g)

    prev = glu(ap_ref[...], gp_ref[...])
    nxt = glu(an_ref[...], gn_ref[...])
    buf_ref[0:CV_HALO, :] = jnp.where(i > 0, prev, 0.0)
    buf_ref[CV_HALO:CV_HALO + ts, :] = glu(a_ref[...], g_ref[...])
    buf_ref[CV_HALO + ts:CV_HALO + ts + CV_HALO, :] = jnp.where(i < n_i - 1, nxt, 0.0)

    span = ts + 2 * CV_HALO - V7X_SUBLANES
    for r in range(V7X_SUBLANES):
        sh_ref[r] = buf_ref[pl.ds(r, span), :]

    base_off = CV_HALO - CV_KERNEL // 2
    dw = dw_ref[...]

    def chunk(c, carry):
        r0 = pl.multiple_of(c * CV_CHUNK, CV_CHUNK)
        acc = jnp.zeros((CV_CHUNK, CV_WIDTH), F32) + db_ref[...]
        for j in range(CV_KERNEL):
            off = base_off + j
            q, r = off // V7X_SUBLANES, off % V7X_SUBLANES
            rows = pl.multiple_of(r0 + q * V7X_SUBLANES, V7X_SUBLANES)
            acc = acc + dw[j:j + 1, :] * sh_ref[r, pl.ds(rows, CV_CHUNK), :]
        conv_ref[pl.ds(r0, CV_CHUNK), :] = acc
        return carry

    lax.fori_loop(0, ts // CV_CHUNK, chunk, 0)

    y = conv_ref[...]
    mu = jnp.mean(y, axis=-1, keepdims=True)
    yc = y - mu
    var = jnp.mean(yc * yc, axis=-1, keepdims=True)
    yn = yc * lax.rsqrt(var + EPS) * lg_ref[...] + lb_ref[...]
    act = yn * jax.nn.sigmoid(yn)
    out = jnp.dot(act.astype(BF16), pwb_ref[...], preferred_element_type=F32) + pb_ref[...]
    ms = jnp.mean(out * out, axis=-1, keepdims=True)
    o_ref[...] = (out * lax.rsqrt(ms + EPS) * gg_ref[...]).astype(o_ref.dtype)


def conformer(z3, dw3, db3, lg3, lb3, pw3, pb3, gg3, layer, name, ts=256):
    b, seq, _ = z3.shape
    ts = min(ts, seq)
    nt = seq // ts
    hb = ts // CV_HALO
    n_hblk = seq // CV_HALO
    main = lambda col: pl.BlockSpec((None, ts, CV_WIDTH), lambda bi, i: (bi, i, col))
    prev = lambda col: pl.BlockSpec((None, CV_HALO, CV_WIDTH),
                                    lambda bi, i: (bi, jnp.maximum(i * hb - 1, 0), col))
    nxt = lambda col: pl.BlockSpec((None, CV_HALO, CV_WIDTH),
                                   lambda bi, i: (bi, jnp.minimum((i + 1) * hb, n_hblk - 1), col))
    vec = lambda rows: pl.BlockSpec((None, rows, CV_WIDTH), lambda bi, i: (layer, 0, 0))
    kern = functools.partial(_conformer_kernel, ts=ts)
    return pl.pallas_call(
        kern,
        out_shape=jax.ShapeDtypeStruct((b, seq, CV_WIDTH), BF16),
        grid=(b, nt),
        in_specs=[
            main(COL_CA), main(COL_CB), prev(COL_CA), prev(COL_CB), nxt(COL_CA), nxt(COL_CB),
            vec(CV_KERNEL), vec(1), vec(1), vec(1),
            pl.BlockSpec((None, CV_WIDTH, CV_WIDTH), lambda bi, i: (layer, 0, 0)),
            vec(1),
            pl.BlockSpec((None, 1, CV_WIDTH), lambda bi, i: (layer, 0, (ATTN_WIDTH + RG_WIDTH) // CV_WIDTH)),
        ],
        out_specs=pl.BlockSpec((None, ts, CV_WIDTH), lambda bi, i: (bi, i, 0)),
        scratch_shapes=[
            pltpu.VMEM((ts + 2 * CV_HALO, CV_WIDTH), F32),
            pltpu.VMEM((V7X_SUBLANES, ts + 2 * CV_HALO - V7X_SUBLANES, CV_WIDTH), F32),
            pltpu.VMEM((ts, CV_WIDTH), F32),
            pltpu.VMEM((CV_WIDTH, CV_WIDTH), BF16),
        ],
        compiler_params=_cparams(("arbitrary", "arbitrary")),
        name=name,
    )(z3, z3, z3, z3, z3, z3, dw3, db3, lg3, lb3, pw3, pb3, gg3)


def kernel(x, mem, g_mix, w_in, g_q, g_k, rg_conv_w, rg_conv_b, rg_w_a, rg_b_a, rg_w_x, rg_b_x, rg_lam, cv_dw_w, cv_dw_b, cv_ln_g, cv_ln_b, cv_pw_w, cv_pw_b, g_grp, w_out, g_xattn, g_mem, xa_wq, xa_wk, xa_wv, xa_wo, g_ffn, ffn_wg, ffn_wu, ffn_wd, g_final):
    b, seq, d = x.shape
    n_mem = mem.shape[1]
    depth = w_in.shape[0]
    m = b * seq
    row3 = lambda p: p.reshape(p.shape[0], 1, p.shape[-1])

    cos_t, sin_t = rope_tables(seq)
    g_mix3, g_q3, g_k3, g_grp3 = row3(g_mix), row3(g_q), row3(g_k), row3(g_grp)
    g_xattn3, g_mem3, g_ffn3 = row3(g_xattn), row3(g_mem), row3(g_ffn)
    g_final3 = g_final.reshape(1, 1, d)
    rg_cb3, cv_db3, cv_lg3, cv_lb3, cv_pb3 = (row3(rg_conv_b), row3(cv_dw_b), row3(cv_ln_g),
                                              row3(cv_ln_b), row3(cv_pw_b))
    rg_w4 = jnp.concatenate([rg_w_a[:, 0], rg_w_x[:, 0], rg_w_a[:, 1], rg_w_x[:, 1]], axis=-1)
    rg_w4 = rg_w4.reshape(-1, RG_BLOCK, 4 * RG_BLOCK)
    wd_bf16 = cast_bf16(ffn_wd, "cast_wd")

    xf = x.reshape(m, d)
    memf = mem.reshape(b * n_mem, d)

    for l in range(depth):
        h = rmsnorm(xf, g_mix3, l, 0, BF16, f"norm_mix{l}")
        z = matmul([h], w_in, l, F32, f"in_proj{l}", tn=512, tm=512)
        z3 = z.reshape(b, seq, z.shape[1])
        q, k, v = qkv_prep(z, g_q3, g_k3, cos_t, sin_t, l, seq, f"qkv_prep{l}")
        y_attn = attention(q, k, v, b, seq, f"attn{l}")
        y_rec = rglru(z3, rg_conv_w, rg_cb3, rg_w4, rg_b_a, rg_b_x, rg_lam, l, f"rglru{l}")
        y_cv_n = conformer(z3, cv_dw_w, cv_db3, cv_lg3, cv_lb3, cv_pw_w, cv_pb3, g_grp3, l, f"conformer{l}")
        y_attn_n = rmsnorm(y_attn, g_grp3, l, 0, BF16, f"norm_attn{l}")
        y_rec_n = rmsnorm(y_rec.reshape(m, RG_WIDTH), g_grp3, l, ATTN_WIDTH // RG_WIDTH, BF16, f"norm_rec{l}")
        xf = matmul([y_attn_n, y_rec_n, y_cv_n.reshape(m, CV_WIDTH)], w_out, l, F32, f"out_proj{l}",
                    tn=512, tm=512, res=xf)

        h = rmsnorm(xf, g_xattn3, l, 0, BF16, f"norm_xattn{l}")
        mn = rmsnorm(memf, g_mem3, l, 0, BF16, f"norm_mem{l}")
        q2 = matmul([h], xa_wq, l, BF16, f"xa_q{l}", tn=512, tm=512)
        k2 = matmul([mn], xa_wk, l, BF16, f"xa_k{l}", tn=512, tm=512)
        v2 = matmul([mn], xa_wv, l, BF16, f"xa_v{l}", tn=512, tm=512)
        o2 = cross_attention(q2, k2, v2, b, seq, n_mem, f"xattn{l}")
        xf = matmul([o2], xa_wo, l, F32, f"xa_o{l}", tn=512, tm=512, res=xf)

        h = rmsnorm(xf, g_ffn3, l, 0, BF16, f"norm_ffn{l}")
        act = swiglu(h, ffn_wg, ffn_wu, l, f"ffn_up{l}", tn=512, tm=512)
        xf = matmul([act], wd_bf16, l, F32, f"ffn_down{l}", tn=512, tm=512, res=xf)

    out = rmsnorm(xf, g_final3, 0, 0, F32, "norm_final")
    return out.reshape(b, seq, d)
```

```python
import functools
import math

import jax
import jax.numpy as jnp
from jax import lax
from jax.experimental import pallas as pl
from jax.experimental.pallas import tpu as pltpu

F32 = jnp.float32
BF16 = jnp.bfloat16

HEAD_DIM = 128
N_Q_HEADS = 16
N_KV_HEADS = 4
Q_PER_KV = N_Q_HEADS // N_KV_HEADS
ATTN_WIDTH = N_Q_HEADS * HEAD_DIM
KV_WIDTH = N_KV_HEADS * HEAD_DIM
RG_WIDTH = 1024
RG_BLOCK = 128
RG_CONV_W = 4
RG_C = 8.0
CV_WIDTH = 1024
CV_KERNEL = 31
QKV_WIDTH = ATTN_WIDTH + 2 * KV_WIDTH
COL_RX = QKV_WIDTH // RG_WIDTH
COL_RG = COL_RX + 1
COL_CA = COL_RG + 1
COL_CB = COL_CA + 1
MEM_HEADS = 4
GRID_W = 64
ROPE_THETA = 10000.0
ROPE_AXIS_DIM = HEAD_DIM // 2
EPS = 1e-6

V7X_VMEM_BYTES = 64 * 1024 * 1024
V7X_SUBLANES = 8
V7X_LANES = 128
VMEM_LIMIT = V7X_VMEM_BYTES - 8 * 1024 * 1024

CV_HALO = 16


def _sigmoid(x):
    return 0.5 * jnp.tanh(0.5 * x) + 0.5


def _cparams(sem):
    return pltpu.CompilerParams(dimension_semantics=sem, vmem_limit_bytes=VMEM_LIMIT)


def _rmsnorm_kernel(x_ref, g_ref, o_ref):
    x = x_ref[...].astype(F32)
    ms = jnp.mean(x * x, axis=-1, keepdims=True)
    o_ref[...] = (x * lax.rsqrt(ms + EPS) * g_ref[...]).astype(o_ref.dtype)


def rmsnorm(x, g3, layer, gcol, out_dtype, name, tm=256):
    m, w = x.shape
    tm = min(tm, m)
    return pl.pallas_call(
        _rmsnorm_kernel,
        out_shape=jax.ShapeDtypeStruct((m, w), out_dtype),
        grid=(m // tm,),
        in_specs=[
            pl.BlockSpec((tm, w), lambda i: (i, 0)),
            pl.BlockSpec((None, 1, w), lambda i: (layer, 0, gcol)),
        ],
        out_specs=pl.BlockSpec((tm, w), lambda i: (i, 0)),
        compiler_params=_cparams(("arbitrary",)),
        name=name,
    )(x, g3)


CAST_ROWS = 256


def _cast_panel(w_ref, wb_ref):
    k = w_ref.shape[0]

    def body(i, c):
        r = pl.multiple_of(i * CAST_ROWS, CAST_ROWS)
        wb_ref[pl.ds(r, CAST_ROWS), :] = w_ref[pl.ds(r, CAST_ROWS), :].astype(BF16)
        return c

    lax.fori_loop(0, k // CAST_ROWS, body, 0)


def _mm_kernel(*refs, k_splits, has_bias, has_res, w_is_bf16):
    n_a = len(k_splits)
    a_refs = refs[:n_a]
    w_ref = refs[n_a]
    pos = n_a + 1
    bias_ref = res_ref = None
    if has_bias:
        bias_ref = refs[pos]
        pos += 1
    if has_res:
        res_ref = refs[pos]
        pos += 1
    o_ref = refs[pos]
    if w_is_bf16:
        wb_ref = w_ref
    else:
        wb_ref = refs[pos + 1]

        @pl.when(pl.program_id(1) == 0)
        def _():
            _cast_panel(w_ref, wb_ref)

    acc = None
    off = 0
    for a_ref, kk in zip(a_refs, k_splits):
        d = jnp.dot(a_ref[...], wb_ref[off:off + kk, :], preferred_element_type=F32)
        acc = d if acc is None else acc + d
        off += kk
    if has_bias:
        acc = acc + bias_ref[...]
    if has_res:
        acc = acc + res_ref[...]
    o_ref[...] = acc.astype(o_ref.dtype)


def matmul(a_list, w3, layer, out_dtype, name, *, tn, tm, bias3=None, res=None):
    m = a_list[0].shape[0]
    k_splits = tuple(a.shape[1] for a in a_list)
    _, k, n = w3.shape
    assert sum(k_splits) == k
    tm = min(tm, m)
    w_is_bf16 = w3.dtype == BF16
    in_specs = [pl.BlockSpec((tm, kk), lambda j, i: (i, 0)) for kk in k_splits]
    in_specs.append(pl.BlockSpec((None, k, tn), lambda j, i: (layer, 0, j)))
    args = list(a_list) + [w3]
    if bias3 is not None:
        in_specs.append(pl.BlockSpec((None, 1, tn), lambda j, i: (layer, 0, j)))
        args.append(bias3)
    if res is not None:
        in_specs.append(pl.BlockSpec((tm, tn), lambda j, i: (i, j)))
        args.append(res)
    scratch = [] if w_is_bf16 else [pltpu.VMEM((k, tn), BF16)]
    kern = functools.partial(_mm_kernel, k_splits=k_splits, has_bias=bias3 is not None,
                             has_res=res is not None, w_is_bf16=w_is_bf16)
    return pl.pallas_call(
        kern,
        out_shape=jax.ShapeDtypeStruct((m, n), out_dtype),
        grid=(n // tn, m // tm),
        in_specs=in_specs,
        out_specs=pl.BlockSpec((tm, tn), lambda j, i: (i, j)),
        scratch_shapes=scratch,
        compiler_params=_cparams(("arbitrary", "arbitrary")),
        name=name,
    )(*args)


def _swiglu_kernel(a_ref, wg0_ref, wg1_ref, wu0_ref, wu1_ref, o_ref, wgb_ref, wub_ref):
    half = wg0_ref.shape[1]

    @pl.when(pl.program_id(1) == 0)
    def _():
        _cast_panel(wg0_ref, wgb_ref.at[:, 0:half])
        _cast_panel(wg1_ref, wgb_ref.at[:, half:2 * half])
        _cast_panel(wu0_ref, wub_ref.at[:, 0:half])
        _cast_panel(wu1_ref, wub_ref.at[:, half:2 * half])

    a = a_ref[...]
    g = jnp.dot(a, wgb_ref[...], preferred_element_type=F32)
    u = jnp.dot(a, wub_ref[...], preferred_element_type=F32)
    o_ref[...] = (g * _sigmoid(g) * u).astype(o_ref.dtype)


def swiglu(a, wg3, wu3, layer, name, *, tn, tm):
    m, k = a.shape
    n = wg3.shape[2]
    tm = min(tm, m)
    half = tn // 2
    n_half = n // half
    assert n % half == 0
    w0 = pl.BlockSpec((None, k, half), lambda j, i: (layer, 0, 2 * j))
    w1 = pl.BlockSpec((None, k, half), lambda j, i: (layer, 0, jnp.minimum(2 * j + 1, n_half - 1)))
    return pl.pallas_call(
        _swiglu_kernel,
        out_shape=jax.ShapeDtypeStruct((m, n), BF16),
        grid=(pl.cdiv(n, tn), m // tm),
        in_specs=[pl.BlockSpec((tm, k), lambda j, i: (i, 0)), w0, w1, w0, w1],
        out_specs=pl.BlockSpec((tm, tn), lambda j, i: (i, j)),
        scratch_shapes=[pltpu.VMEM((k, tn), BF16), pltpu.VMEM((k, tn), BF16)],
        compiler_params=_cparams(("arbitrary", "arbitrary")),
        name=name,
    )(a, wg3, wg3, wu3, wu3)


def _cast_kernel(x_ref, o_ref):
    o_ref[...] = x_ref[...].astype(o_ref.dtype)


def cast_bf16(w3, name, rows=256):
    l, k, n = w3.shape
    return pl.pallas_call(
        _cast_kernel,
        out_shape=jax.ShapeDtypeStruct((l, k, n), BF16),
        grid=(l, k // rows),
        in_specs=[pl.BlockSpec((None, rows, n), lambda a, i: (a, i, 0))],
        out_specs=pl.BlockSpec((None, rows, n), lambda a, i: (a, i, 0)),
        compiler_params=_cparams(("arbitrary", "arbitrary")),
        name=name,
    )(w3)


def rope_tables(seq):
    rows = seq // GRID_W
    row = jnp.repeat(jnp.arange(rows), GRID_W).astype(F32)
    col = jnp.tile(jnp.arange(GRID_W), rows).astype(F32)
    inv = ROPE_THETA ** (-jnp.arange(0, ROPE_AXIS_DIM, 2, dtype=F32) / ROPE_AXIS_DIM)
    ang_r = row[:, None] * inv
    ang_c = col[:, None] * inv
    cos_t = jnp.concatenate([jnp.cos(ang_r), jnp.cos(ang_r), jnp.cos(ang_c), jnp.cos(ang_c)], axis=1)
    sin_t = jnp.concatenate([-jnp.sin(ang_r), jnp.sin(ang_r), -jnp.sin(ang_c), jnp.sin(ang_c)], axis=1)
    return cos_t, sin_t


def _norm_rope(x, g, cos, sin):
    half = ROPE_AXIS_DIM // 2
    lane = lax.broadcasted_iota(jnp.int32, (1, HEAD_DIM), 1)
    first_half = (lane % ROPE_AXIS_DIM) < half
    ms = jnp.mean(x * x, axis=-1, keepdims=True)
    xn = x * lax.rsqrt(ms + EPS) * g
    swapped = jnp.where(first_half, pltpu.roll(xn, HEAD_DIM - half, axis=1), pltpu.roll(xn, half, axis=1))
    return xn * cos + swapped * sin


def _attn_kernel(zq_ref, zk_ref, zv_ref, gq_ref, gk_ref, cos_ref, sin_ref, o_ref, k_s, v_s, *, tq, seq):
    i = pl.program_id(2)

    @pl.when(i == 0)
    def _():
        k_s[...] = _norm_rope(zk_ref[...], gk_ref[...], cos_ref[...], sin_ref[...]).astype(BF16)
        v_s[:, 0:HEAD_DIM] = zv_ref[...].astype(BF16)
        v_s[:, HEAD_DIM:2 * HEAD_DIM] = jnp.ones((seq, HEAD_DIM), BF16)

    r0 = pl.multiple_of(i * tq, tq)
    cos = cos_ref[pl.ds(r0, tq), :]
    sin = sin_ref[pl.ds(r0, tq), :]
    gq = gq_ref[...]
    k = k_s[...]
    v = v_s[...]
    c = (HEAD_DIM ** -0.5) * math.log2(math.e)
    for h in range(Q_PER_KV):
        c0 = h * HEAD_DIM
        q = _norm_rope(zq_ref[:, c0:c0 + HEAD_DIM], gq, cos, sin).astype(BF16)
        s = lax.dot_general(q, k, (((1,), (1,)), ((), ())), preferred_element_type=F32)
        mx = jnp.max(s, axis=-1, keepdims=True)
        p = jnp.exp2((s - mx) * c).astype(BF16)
        o = jnp.dot(p, v, preferred_element_type=F32)
        o_ref[:, c0:c0 + HEAD_DIM] = (o[:, 0:HEAD_DIM] / o[:, HEAD_DIM:2 * HEAD_DIM]).astype(o_ref.dtype)


def attention(z, gq3, gk3, cos_t, sin_t, layer, batch, seq, name, tq=256):
    m = z.shape[0]
    tq = min(tq, seq)
    nq = seq // tq
    gw = Q_PER_KV * HEAD_DIM
    kcol = ATTN_WIDTH // HEAD_DIM
    vcol = (ATTN_WIDTH + KV_WIDTH) // HEAD_DIM
    gspec = pl.BlockSpec((None, 1, HEAD_DIM), lambda b, g, i: (layer, 0, 0))
    tspec = pl.BlockSpec((seq, HEAD_DIM), lambda b, g, i: (0, 0))
    kern = functools.partial(_attn_kernel, tq=tq, seq=seq)
    return pl.pallas_call(
        kern,
        out_shape=jax.ShapeDtypeStruct((m, ATTN_WIDTH), F32),
        grid=(batch, N_KV_HEADS, nq),
        in_specs=[
            pl.BlockSpec((tq, gw), lambda b, g, i: (b * nq + i, g)),
            pl.BlockSpec((seq, HEAD_DIM), lambda b, g, i: (b, kcol + g)),
            pl.BlockSpec((seq, HEAD_DIM), lambda b, g, i: (b, vcol + g)),
            gspec, gspec, tspec, tspec,
        ],
        out_specs=pl.BlockSpec((tq, gw), lambda b, g, i: (b * nq + i, g)),
        scratch_shapes=[pltpu.VMEM((seq, HEAD_DIM), BF16), pltpu.VMEM((seq, 2 * HEAD_DIM), BF16)],
        compiler_params=_cparams(("arbitrary", "arbitrary", "arbitrary")),
        name=name,
    )(z, z, z, gq3, gk3, cos_t, sin_t)


def _xattn_kernel(q_ref, k_ref, v_ref, o_ref, *, head_dim):
    scale = head_dim ** -0.5
    for h in range(MEM_HEADS):
        c0 = h * head_dim
        q = q_ref[:, c0:c0 + head_dim]
        k = k_ref[:, c0:c0 + head_dim]
        v = v_ref[:, c0:c0 + head_dim]
        s = lax.dot_general(q, k, (((1,), (1,)), ((), ())), preferred_element_type=F32) * scale
        mx = jnp.max(s, axis=-1, keepdims=True)
        p = jnp.exp(s - mx)
        den = jnp.sum(p, axis=-1, keepdims=True)
        o = jnp.dot(p.astype(v.dtype), v, preferred_element_type=F32)
        o_ref[:, c0:c0 + head_dim] = (o / den).astype(o_ref.dtype)


def cross_attention(q, k, v, batch, seq, n_mem, name, tq=512):
    m, d = q.shape
    tq = min(tq, seq)
    nq = seq // tq
    kern = functools.partial(_xattn_kernel, head_dim=d // MEM_HEADS)
    return pl.pallas_call(
        kern,
        out_shape=jax.ShapeDtypeStruct((m, d), BF16),
        grid=(batch, nq),
        in_specs=[
            pl.BlockSpec((tq, d), lambda b, i: (b * nq + i, 0)),
            pl.BlockSpec((n_mem, d), lambda b, i: (b, 0)),
            pl.BlockSpec((n_mem, d), lambda b, i: (b, 0)),
        ],
        out_specs=pl.BlockSpec((tq, d), lambda b, i: (b * nq + i, 0)),
        compiler_params=_cparams(("arbitrary", "arbitrary")),
        name=name,
    )(q, k, v)


RG_PAD = V7X_SUBLANES


def _rglru_kernel(x_ref, gate_ref, cw_ref, cb_ref, w_ref, ba_ref, bx_ref, lam_ref, y_ref,
                  pad_ref, af_ref, uf_ref, ab_ref, ub_ref, *, seq, heads):
    cw = heads * RG_BLOCK
    zeros = jnp.zeros((RG_PAD, cw), F32)
    pad_ref[0:RG_PAD, :] = zeros
    pad_ref[RG_PAD + seq:RG_PAD + seq + RG_PAD, :] = zeros
    pad_ref[RG_PAD:RG_PAD + seq, :] = x_ref[...]
    left = RG_CONV_W // 2
    xc = cb_ref[...] + cw_ref[0:1, :] * pad_ref[pl.ds(RG_PAD - left, seq), :]
    for j in range(1, RG_CONV_W):
        xc = xc + cw_ref[j:j + 1, :] * pad_ref[pl.ds(RG_PAD - left + j, seq), :]
    xcb = xc.astype(BF16)

    lam = lam_ref[...]
    nl = -lam
    softplus = jnp.maximum(nl, 0.0) + jnp.log1p(jnp.exp(-jnp.abs(nl)))
    ba = ba_ref[...]
    bx = bx_ref[...]
    for h in range(heads):
        c0 = h * RG_BLOCK
        sl = slice(c0, c0 + RG_BLOCK)
        gates = jnp.dot(xcb[:, sl], w_ref[h].astype(BF16), preferred_element_type=F32)
        xch = xc[:, sl]
        for d, (a_ref, u_ref) in enumerate(((af_ref, uf_ref), (ab_ref, ub_ref))):
            g0 = 2 * d * RG_BLOCK
            r = _sigmoid(gates[:, g0:g0 + RG_BLOCK] + ba[d:d + 1, sl])
            i = _sigmoid(gates[:, g0 + RG_BLOCK:g0 + 2 * RG_BLOCK] + bx[d:d + 1, sl])
            log_a = (-RG_C) * r * softplus[d:d + 1, sl]
            t = jnp.tanh(log_a)
            one_minus_a2 = (-2.0 * t) / (1.0 - t)
            a_ref[:, sl] = jnp.exp(log_a)
            u_ref[:, sl] = jnp.sqrt(one_minus_a2) * (i * xch)

    n_tiles = seq // V7X_SUBLANES
    row = lax.broadcasted_iota(jnp.int32, (V7X_SUBLANES, cw), 0)

    def tile_scan(a, u, reverse):
        for d in (1, 2, 4):
            if reverse:
                shift, keep = V7X_SUBLANES - d, row < V7X_SUBLANES - d
            else:
                shift, keep = d, row >= d
            a_s = pltpu.roll(a, shift, axis=0)
            u_s = pltpu.roll(u, shift, axis=0)
            u = jnp.where(keep, u + a * u_s, u)
            a = jnp.where(keep, a * a_s, a)
        return a, u

    def body(t, carry):
        hf, hb = carry
        rf = pl.multiple_of(t * V7X_SUBLANES, V7X_SUBLANES)
        a, u = tile_scan(af_ref[pl.ds(rf, V7X_SUBLANES), :], uf_ref[pl.ds(rf, V7X_SUBLANES), :], False)
        hh = a * hf + u
        uf_ref[pl.ds(rf, V7X_SUBLANES), :] = hh
        hf = jnp.broadcast_to(hh[V7X_SUBLANES - 1:V7X_SUBLANES, :], hh.shape)
        rb = pl.multiple_of((n_tiles - 1 - t) * V7X_SUBLANES, V7X_SUBLANES)
        a, u = tile_scan(ab_ref[pl.ds(rb, V7X_SUBLANES), :], ub_ref[pl.ds(rb, V7X_SUBLANES), :], True)
        hh = a * hb + u
        ub_ref[pl.ds(rb, V7X_SUBLANES), :] = hh
        hb = jnp.broadcast_to(hh[0:1, :], hh.shape)
        return hf, hb

    h0 = jnp.zeros((V7X_SUBLANES, cw), F32)
    lax.fori_loop(0, n_tiles, body, (h0, h0))

    g = gate_ref[...]
    gelu = 0.5 * g * (1.0 + jnp.tanh(math.sqrt(2.0 / math.pi) * (g + 0.044715 * (g * g * g))))
    y_ref[...] = ((uf_ref[...] + ub_ref[...]) * gelu).astype(y_ref.dtype)


def rglru(z3, cw3, cb3, w4, ba3, bx3, lam3, layer, name, heads=2):
    b, seq, _ = z3.shape
    cw = heads * RG_BLOCK
    ncb = RG_WIDTH // cw
    n_rg_heads = RG_WIDTH // RG_BLOCK
    kern = functools.partial(_rglru_kernel, seq=seq, heads=heads)
    vec = lambda rows: pl.BlockSpec((None, rows, cw), lambda bi, j: (layer, 0, j))
    return pl.pallas_call(
        kern,
        out_shape=jax.ShapeDtypeStruct((b, seq, RG_WIDTH), F32),
        grid=(b, ncb),
        in_specs=[
            pl.BlockSpec((None, seq, cw), lambda bi, j: (bi, 0, COL_RX * ncb + j)),
            pl.BlockSpec((None, seq, cw), lambda bi, j: (bi, 0, COL_RG * ncb + j)),
            vec(RG_CONV_W), vec(1),
            pl.BlockSpec((heads, RG_BLOCK, 4 * RG_BLOCK), lambda bi, j: ((layer * n_rg_heads) // heads + j, 0, 0)),
            vec(2), vec(2), vec(2),
        ],
        out_specs=pl.BlockSpec((None, seq, cw), lambda bi, j: (bi, 0, j)),
        scratch_shapes=[pltpu.VMEM((seq + 2 * RG_PAD, cw), F32)] + [pltpu.VMEM((seq, cw), F32)] * 4,
        compiler_params=_cparams(("arbitrary", "arbitrary")),
        name=name,
    )(z3, z3, cw3, cb3, w4, ba3, bx3, lam3)


CV_CHUNK = 32


def _conformer_kernel(a_ref, g_ref, ap_ref, gp_ref, an_ref, gn_ref, dw_ref, db_ref, lg_ref, lb_ref,
                      pw_ref, pb_ref, gg_ref, o_ref, buf_ref, sh_ref, conv_ref, pwb_ref, dwb_ref, *, ts):
    i = pl.program_id(1)
    n_i = pl.num_programs(1)

    @pl.when(jnp.logical_and(pl.program_id(0) == 0, i == 0))
    def _():
        pwb_ref[...] = pw_ref[...].astype(BF16)
        for j in range(CV_KERNEL):
            dwb_ref[j] = jnp.broadcast_to(dw_ref[j:j + 1, :], (V7X_SUBLANES, CV_WIDTH))

    def glu(a, g):
        return a * _sigmoid(g)

    prev = glu(ap_ref[...], gp_ref[...])
    nxt = glu(an_ref[...], gn_ref[...])
    buf_ref[0:CV_HALO, :] = jnp.where(i > 0, prev, 0.0)
    buf_ref[CV_HALO:CV_HALO + ts, :] = glu(a_ref[...], g_ref[...])
    buf_ref[CV_HALO + ts:CV_HALO + ts + CV_HALO, :] = jnp.where(i < n_i - 1, nxt, 0.0)

    span = ts + 2 * CV_HALO - V7X_SUBLANES
    for r in range(V7X_SUBLANES):
        sh_ref[r] = buf_ref[pl.ds(r, span), :]

    base_off = CV_HALO - CV_KERNEL // 2
    n_sub = CV_CHUNK // V7X_SUBLANES
    bias = jnp.broadcast_to(db_ref[...], (V7X_SUBLANES, CV_WIDTH))

    def chunk(c, carry):
        r0 = pl.multiple_of(c * CV_CHUNK, CV_CHUNK)
        accs = [bias] * n_sub
        for j in range(CV_KERNEL):
            off = base_off + j
            q, r = off // V7X_SUBLANES, off % V7X_SUBLANES
            w = dwb_ref[j]
            for s in range(n_sub):
                rows = pl.multiple_of(r0 + (q + s) * V7X_SUBLANES, V7X_SUBLANES)
                accs[s] = accs[s] + w * sh_ref[r, pl.ds(rows, V7X_SUBLANES), :]
        for s in range(n_sub):
            rows = pl.multiple_of(r0 + s * V7X_SUBLANES, V7X_SUBLANES)
            conv_ref[pl.ds(rows, V7X_SUBLANES), :] = accs[s]
        return carry

    lax.fori_loop(0, ts // CV_CHUNK, chunk, 0)

    y = conv_ref[...]
    mu = jnp.mean(y, axis=-1, keepdims=True)
    yc = y - mu
    var = jnp.mean(yc * yc, axis=-1, keepdims=True)
    yn = yc * lax.rsqrt(var + EPS) * lg_ref[...] + lb_ref[...]
    act = yn * _sigmoid(yn)
    out = jnp.dot(act.astype(BF16), pwb_ref[...], preferred_element_type=F32) + pb_ref[...]
    ms = jnp.mean(out * out, axis=-1, keepdims=True)
    o_ref[...] = (out * lax.rsqrt(ms + EPS) * gg_ref[...]).astype(o_ref.dtype)


def conformer(z3, dw3, db3, lg3, lb3, pw3, pb3, gg3, layer, name, ts=256):
    b, seq, _ = z3.shape
    ts = min(ts, seq)
    nt = seq // ts
    hb = ts // CV_HALO
    n_hblk = seq // CV_HALO
    main = lambda col: pl.BlockSpec((None, ts, CV_WIDTH), lambda bi, i: (bi, i, col))
    prev = lambda col: pl.BlockSpec((None, CV_HALO, CV_WIDTH),
                                    lambda bi, i: (bi, jnp.maximum(i * hb - 1, 0), col))
    nxt = lambda col: pl.BlockSpec((None, CV_HALO, CV_WIDTH),
                                   lambda bi, i: (bi, jnp.minimum((i + 1) * hb, n_hblk - 1), col))
    vec = lambda rows: pl.BlockSpec((None, rows, CV_WIDTH), lambda bi, i: (layer, 0, 0))
    kern = functools.partial(_conformer_kernel, ts=ts)
    return pl.pallas_call(
        kern,
        out_shape=jax.ShapeDtypeStruct((b, seq, CV_WIDTH), BF16),
        grid=(b, nt),
        in_specs=[
            main(COL_CA), main(COL_CB), prev(COL_CA), prev(COL_CB), nxt(COL_CA), nxt(COL_CB),
            vec(CV_KERNEL), vec(1), vec(1), vec(1),
            pl.BlockSpec((None, CV_WIDTH, CV_WIDTH), lambda bi, i: (layer, 0, 0)),
            vec(1),
            pl.BlockSpec((None, 1, CV_WIDTH), lambda bi, i: (layer, 0, (ATTN_WIDTH + RG_WIDTH) // CV_WIDTH)),
        ],
        out_specs=pl.BlockSpec((None, ts, CV_WIDTH), lambda bi, i: (bi, i, 0)),
        scratch_shapes=[
            pltpu.VMEM((ts + 2 * CV_HALO, CV_WIDTH), F32),
            pltpu.VMEM((V7X_SUBLANES, ts + 2 * CV_HALO - V7X_SUBLANES, CV_WIDTH), F32),
            pltpu.VMEM((ts, CV_WIDTH), F32),
            pltpu.VMEM((CV_WIDTH, CV_WIDTH), BF16),
            pltpu.VMEM((CV_KERNEL, V7X_SUBLANES, CV_WIDTH), F32),
        ],
        compiler_params=_cparams(("arbitrary", "arbitrary")),
        name=name,
    )(z3, z3, z3, z3, z3, z3, dw3, db3, lg3, lb3, pw3, pb3, gg3)


def kernel(x, mem, g_mix, w_in, g_q, g_k, rg_conv_w, rg_conv_b, rg_w_a, rg_b_a, rg_w_x, rg_b_x, rg_lam, cv_dw_w, cv_dw_b, cv_ln_g, cv_ln_b, cv_pw_w, cv_pw_b, g_grp, w_out, g_xattn, g_mem, xa_wq, xa_wk, xa_wv, xa_wo, g_ffn, ffn_wg, ffn_wu, ffn_wd, g_final):
    b, seq, d = x.shape
    n_mem = mem.shape[1]
    depth = w_in.shape[0]
    m = b * seq
    row3 = lambda p: p.reshape(p.shape[0], 1, p.shape[-1])

    cos_t, sin_t = rope_tables(seq)
    g_mix3, g_q3, g_k3, g_grp3 = row3(g_mix), row3(g_q), row3(g_k), row3(g_grp)
    g_xattn3, g_mem3, g_ffn3 = row3(g_xattn), row3(g_mem), row3(g_ffn)
    g_final3 = g_final.reshape(1, 1, d)
    rg_cb3, cv_db3, cv_lg3, cv_lb3, cv_pb3 = (row3(rg_conv_b), row3(cv_dw_b), row3(cv_ln_g),
                                              row3(cv_ln_b), row3(cv_pw_b))
    rg_w4 = jnp.concatenate([rg_w_a[:, 0], rg_w_x[:, 0], rg_w_a[:, 1], rg_w_x[:, 1]], axis=-1)
    rg_w4 = rg_w4.reshape(-1, RG_BLOCK, 4 * RG_BLOCK)
    wd_bf16 = cast_bf16(ffn_wd, "cast_wd")

    xf = x.reshape(m, d)
    memf = mem.reshape(b * n_mem, d)

    for l in range(depth):
        h = rmsnorm(xf, g_mix3, l, 0, BF16, f"norm_mix{l}")
        z = matmul([h], w_in, l, F32, f"in_proj{l}", tn=512, tm=1024)
        z3 = z.reshape(b, seq, z.shape[1])
        y_attn = attention(z, g_q3, g_k3, cos_t, sin_t, l, b, seq, f"attn{l}")
        y_rec = rglru(z3, rg_conv_w, rg_cb3, rg_w4, rg_b_a, rg_b_x, rg_lam, l, f"rglru{l}")
        y_cv_n = conformer(z3, cv_dw_w, cv_db3, cv_lg3, cv_lb3, cv_pw_w, cv_pb3, g_grp3, l, f"conformer{l}")
        y_attn_n = rmsnorm(y_attn, g_grp3, l, 0, BF16, f"norm_attn{l}")
        y_rec_n = rmsnorm(y_rec.reshape(m, RG_WIDTH), g_grp3, l, ATTN_WIDTH // RG_WIDTH, BF16, f"norm_rec{l}")
        xf = matmul([y_attn_n, y_rec_n, y_cv_n.reshape(m, CV_WIDTH)], w_out, l, F32, f"out_proj{l}",
                    tn=512, tm=1024, res=xf)

        h = rmsnorm(xf, g_xattn3, l, 0, BF16, f"norm_xattn{l}")
        mn = rmsnorm(memf, g_mem3, l, 0, BF16, f"norm_mem{l}")
        q2 = matmul([h], xa_wq, l, BF16, f"xa_q{l}", tn=512, tm=1024)
        k2 = matmul([mn], xa_wk, l, BF16, f"xa_k{l}", tn=512, tm=1024)
        v2 = matmul([mn], xa_wv, l, BF16, f"xa_v{l}", tn=512, tm=1024)
        o2 = cross_attention(q2, k2, v2, b, seq, n_mem, f"xattn{l}")
        xf = matmul([o2], xa_wo, l, F32, f"xa_o{l}", tn=512, tm=1024, res=xf)

        h = rmsnorm(xf, g_ffn3, l, 0, BF16, f"norm_ffn{l}")
        act = swiglu(h, ffn_wg, ffn_wu, l, f"ffn_up{l}", tn=256, tm=1024)
        xf = matmul([act], wd_bf16, l, F32, f"ffn_down{l}", tn=512, tm=512, res=xf)

    out = rmsnorm(xf, g_final3, 0, 0, F32, "norm_final")
    return out.reshape(b, seq, d)
```

```python
import functools
import math

import jax
import jax.numpy as jnp
from jax import lax
from jax.experimental import pallas as pl
from jax.experimental.pallas import tpu as pltpu

F32 = jnp.float32
BF16 = jnp.bfloat16

HEAD_DIM = 128
N_Q_HEADS = 16
N_KV_HEADS = 4
Q_PER_KV = N_Q_HEADS // N_KV_HEADS
ATTN_WIDTH = N_Q_HEADS * HEAD_DIM
KV_WIDTH = N_KV_HEADS * HEAD_DIM
RG_WIDTH = 1024
RG_BLOCK = 128
RG_CONV_W = 4
RG_C = 8.0
CV_WIDTH = 1024
CV_KERNEL = 31
QKV_WIDTH = ATTN_WIDTH + 2 * KV_WIDTH
COL_RX = QKV_WIDTH // RG_WIDTH
COL_RG = COL_RX + 1
COL_CA = COL_RG + 1
COL_CB = COL_CA + 1
MEM_HEADS = 4
GRID_W = 64
ROPE_THETA = 10000.0
ROPE_AXIS_DIM = HEAD_DIM // 2
EPS = 1e-6

V7X_VMEM_BYTES = 64 * 1024 * 1024
V7X_SUBLANES = 8
V7X_LANES = 128
VMEM_LIMIT = V7X_VMEM_BYTES - 4 * 1024 * 1024

CV_HALO = 16


def _sigmoid(x):
    return 0.5 * jnp.tanh(0.5 * x) + 0.5


def _cparams(sem):
    return pltpu.CompilerParams(dimension_semantics=sem, vmem_limit_bytes=VMEM_LIMIT)


def _rmsnorm_kernel(x_ref, g_ref, o_ref):
    x = x_ref[...].astype(F32)
    ms = jnp.mean(x * x, axis=-1, keepdims=True)
    o_ref[...] = (x * lax.rsqrt(ms + EPS) * g_ref[...]).astype(o_ref.dtype)


def rmsnorm(x, g3, layer, gcol, out_dtype, name, tm=256):
    m, w = x.shape
    tm = min(tm, m)
    return pl.pallas_call(
        _rmsnorm_kernel,
        out_shape=jax.ShapeDtypeStruct((m, w), out_dtype),
        grid=(m // tm,),
        in_specs=[
            pl.BlockSpec((tm, w), lambda i: (i, 0)),
            pl.BlockSpec((None, 1, w), lambda i: (layer, 0, gcol)),
        ],
        out_specs=pl.BlockSpec((tm, w), lambda i: (i, 0)),
        compiler_params=_cparams(("arbitrary",)),
        name=name,
    )(x, g3)


CAST_ROWS = 256


def _cast_panel(w_ref, wb_ref):
    k = w_ref.shape[0]

    def body(i, c):
        r = pl.multiple_of(i * CAST_ROWS, CAST_ROWS)
        wb_ref[pl.ds(r, CAST_ROWS), :] = w_ref[pl.ds(r, CAST_ROWS), :].astype(BF16)
        return c

    lax.fori_loop(0, k // CAST_ROWS, body, 0)


def _mm_kernel(a_ref, w_ref, o_ref, wb_ref):
    @pl.when(pl.program_id(1) == 0)
    def _():
        _cast_panel(w_ref, wb_ref)

    o_ref[...] = jnp.dot(a_ref[...], wb_ref[...], preferred_element_type=F32).astype(o_ref.dtype)


def matmul(a, w3, layer, out_dtype, name, *, tn, tm):
    m, k = a.shape
    n = w3.shape[2]
    tm = min(tm, m)
    return pl.pallas_call(
        _mm_kernel,
        out_shape=jax.ShapeDtypeStruct((m, n), out_dtype),
        grid=(n // tn, m // tm),
        in_specs=[pl.BlockSpec((tm, k), lambda j, i: (i, 0)),
                  pl.BlockSpec((None, k, tn), lambda j, i: (layer, 0, j))],
        out_specs=pl.BlockSpec((tm, tn), lambda j, i: (i, j)),
        scratch_shapes=[pltpu.VMEM((k, tn), BF16)],
        compiler_params=_cparams(("arbitrary", "arbitrary")),
        name=name,
    )(a, w3)


class _PanelStream:
    def __init__(self, w_hbm, wb_ref, stage_ref, sem, *, layer, n_panels, n_chunks, last_width):
        self.w_hbm, self.wb_ref, self.stage_ref, self.sem = w_hbm, wb_ref, stage_ref, sem
        self.layer, self.n_panels, self.n_chunks = layer, n_panels, n_chunks
        self.ch_rows, self.tn = stage_ref.shape
        self.last_width = last_width

    def _copy(self, panel, chunk, width):
        rows = pl.ds(pl.multiple_of(chunk * self.ch_rows, V7X_SUBLANES), self.ch_rows)
        cols = pl.ds(pl.multiple_of(panel * self.tn, V7X_LANES), width)
        dst = self.stage_ref if width == self.tn else self.stage_ref.at[:, 0:width]
        return pltpu.make_async_copy(self.w_hbm.at[self.layer, rows, cols], dst, self.sem)

    def _cast(self, slot, chunk, width):
        rows = pl.ds(pl.multiple_of(chunk * self.ch_rows, 2 * V7X_SUBLANES), self.ch_rows)
        self.wb_ref[slot, rows, 0:width] = self.stage_ref[:, 0:width].astype(BF16)

    def load_first_panel(self):
        def body(c, carry):
            cp = self._copy(0, c, self.tn)
            cp.start()
            cp.wait()
            self._cast(0, c, self.tn)
            return carry

        lax.fori_loop(0, self.n_chunks, body, 0)

    def _next_cases(self, j):
        nxt = j + 1
        if self.last_width == self.tn:
            return [(nxt < self.n_panels, self.tn)]
        return [(nxt < self.n_panels - 1, self.tn), (nxt == self.n_panels - 1, self.last_width)]

    def start_next(self, j, i):
        for pred, width in self._next_cases(j):
            @pl.when(pred)
            def _(width=width):
                self._copy(j + 1, i, width).start()

    def finish_next(self, j, i):
        for pred, width in self._next_cases(j):
            @pl.when(pred)
            def _(width=width):
                self._copy(j + 1, i, width).wait()
                self._cast((j + 1) % 2, i, width)


def _mm_stream_kernel(*refs, k_splits, has_res, layer, n_panels, n_chunks, last_width):
    n_a = len(k_splits)
    a_refs = refs[:n_a]
    w_hbm = refs[n_a]
    pos = n_a + 1
    res_ref = None
    if has_res:
        res_ref = refs[pos]
        pos += 1
    o_ref, wb_ref, stage_ref, sem = refs[pos:pos + 4]
    j = pl.program_id(0)
    i = pl.program_id(1)
    ps = _PanelStream(w_hbm, wb_ref, stage_ref, sem, layer=layer, n_panels=n_panels, n_chunks=n_chunks,
                      last_width=last_width)

    @pl.when(jnp.logical_and(j == 0, i == 0))
    def _():
        ps.load_first_panel()

    ps.start_next(j, i)
    wcur = wb_ref.at[j % 2]
    acc = None
    off = 0
    for a_ref, kk in zip(a_refs, k_splits):
        d = jnp.dot(a_ref[...], wcur[off:off + kk, :], preferred_element_type=F32)
        acc = d if acc is None else acc + d
        off += kk
    if has_res:
        acc = acc + res_ref[...]
    o_ref[...] = acc.astype(o_ref.dtype)
    ps.finish_next(j, i)


def matmul_stream(a_list, w3, layer, out_dtype, name, *, tn, tm, res=None):
    m = a_list[0].shape[0]
    k_splits = tuple(a.shape[1] for a in a_list)
    _, k, n = w3.shape
    assert sum(k_splits) == k and n % tn == 0 and m % tm == 0
    n_chunks = m // tm
    assert k % (n_chunks * 2 * V7X_SUBLANES) == 0 and n // tn >= 2
    in_specs = [pl.BlockSpec((tm, kk), lambda j, i: (i, 0)) for kk in k_splits]
    in_specs.append(pl.BlockSpec(memory_space=pl.ANY))
    args = list(a_list) + [w3]
    if res is not None:
        in_specs.append(pl.BlockSpec((tm, tn), lambda j, i: (i, j)))
        args.append(res)
    kern = functools.partial(_mm_stream_kernel, k_splits=k_splits, has_res=res is not None, layer=layer,
                             n_panels=n // tn, n_chunks=n_chunks, last_width=tn)
    return pl.pallas_call(
        kern,
        out_shape=jax.ShapeDtypeStruct((m, n), out_dtype),
        grid=(n // tn, n_chunks),
        in_specs=in_specs,
        out_specs=pl.BlockSpec((tm, tn), lambda j, i: (i, j)),
        scratch_shapes=[pltpu.VMEM((2, k, tn), BF16), pltpu.VMEM((k // n_chunks, tn), F32),
                        pltpu.SemaphoreType.DMA(())],
        compiler_params=_cparams(("arbitrary", "arbitrary")),
        name=name,
    )(*args)


def _swiglu_stream_kernel(a_ref, wg_hbm, wu_hbm, o_ref, wgb_ref, wub_ref, sg_ref, su_ref, sems,
                          *, layer, n_panels, n_chunks, last_width):
    j = pl.program_id(0)
    i = pl.program_id(1)
    kw = dict(layer=layer, n_panels=n_panels, n_chunks=n_chunks, last_width=last_width)
    pg = _PanelStream(wg_hbm, wgb_ref, sg_ref, sems.at[0], **kw)
    pu = _PanelStream(wu_hbm, wub_ref, su_ref, sems.at[1], **kw)

    @pl.when(jnp.logical_and(j == 0, i == 0))
    def _():
        pg.load_first_panel()
        pu.load_first_panel()

    pg.start_next(j, i)
    pu.start_next(j, i)
    a = a_ref[...]
    g = jnp.dot(a, wgb_ref[j % 2], preferred_element_type=F32)
    u = jnp.dot(a, wub_ref[j % 2], preferred_element_type=F32)
    o_ref[...] = (g * _sigmoid(g) * u).astype(o_ref.dtype)
    pg.finish_next(j, i)
    pu.finish_next(j, i)


def swiglu_stream(a, wg3, wu3, layer, name, *, tn, tm):
    m, k = a.shape
    n = wg3.shape[2]
    n_panels = pl.cdiv(n, tn)
    n_chunks = m // tm
    last_width = n - (n_panels - 1) * tn
    assert m % tm == 0 and k % (n_chunks * 2 * V7X_SUBLANES) == 0 and n_panels >= 2
    assert last_width % V7X_LANES == 0
    kern = functools.partial(_swiglu_stream_kernel, layer=layer, n_panels=n_panels, n_chunks=n_chunks,
                             last_width=last_width)
    anyspec = pl.BlockSpec(memory_space=pl.ANY)
    return pl.pallas_call(
        kern,
        out_shape=jax.ShapeDtypeStruct((m, n), BF16),
        grid=(n_panels, n_chunks),
        in_specs=[pl.BlockSpec((tm, k), lambda j, i: (i, 0)), anyspec, anyspec],
        out_specs=pl.BlockSpec((tm, tn), lambda j, i: (i, j)),
        scratch_shapes=[pltpu.VMEM((2, k, tn), BF16), pltpu.VMEM((2, k, tn), BF16),
                        pltpu.VMEM((k // n_chunks, tn), F32), pltpu.VMEM((k // n_chunks, tn), F32),
                        pltpu.SemaphoreType.DMA((2,))],
        compiler_params=_cparams(("arbitrary", "arbitrary")),
        name=name,
    )(a, wg3, wu3)


def rope_tables(seq):
    rows = seq // GRID_W
    row = jnp.repeat(jnp.arange(rows), GRID_W).astype(F32)
    col = jnp.tile(jnp.arange(GRID_W), rows).astype(F32)
    inv = ROPE_THETA ** (-jnp.arange(0, ROPE_AXIS_DIM, 2, dtype=F32) / ROPE_AXIS_DIM)
    ang_r = row[:, None] * inv
    ang_c = col[:, None] * inv
    cos_t = jnp.concatenate([jnp.cos(ang_r), jnp.cos(ang_r), jnp.cos(ang_c), jnp.cos(ang_c)], axis=1)
    sin_t = jnp.concatenate([-jnp.sin(ang_r), jnp.sin(ang_r), -jnp.sin(ang_c), jnp.sin(ang_c)], axis=1)
    return cos_t, sin_t


def _norm_rope(x, g, cos, sin):
    half = ROPE_AXIS_DIM // 2
    lane = lax.broadcasted_iota(jnp.int32, (1, HEAD_DIM), 1)
    first_half = (lane % ROPE_AXIS_DIM) < half
    ms = jnp.mean(x * x, axis=-1, keepdims=True)
    xn = x * lax.rsqrt(ms + EPS) * g
    swapped = jnp.where(first_half, pltpu.roll(xn, HEAD_DIM - half, axis=1), pltpu.roll(xn, half, axis=1))
    return xn * cos + swapped * sin


def _attn_kernel(zq_ref, zk_ref, zv_ref, gq_ref, gk_ref, cos_ref, sin_ref, o_ref, k_s, v_s, *, tq, seq):
    i = pl.program_id(2)

    @pl.when(i == 0)
    def _():
        k_s[...] = _norm_rope(zk_ref[...], gk_ref[...], cos_ref[...], sin_ref[...]).astype(BF16)
        v_s[:, 0:HEAD_DIM] = zv_ref[...].astype(BF16)
        v_s[:, HEAD_DIM:2 * HEAD_DIM] = jnp.ones((seq, HEAD_DIM), BF16)

    r0 = pl.multiple_of(i * tq, tq)
    cos = cos_ref[pl.ds(r0, tq), :]
    sin = sin_ref[pl.ds(r0, tq), :]
    gq = gq_ref[...]
    k = k_s[...]
    v = v_s[...]
    c = (HEAD_DIM ** -0.5) * math.log2(math.e)
    for h in range(Q_PER_KV):
        c0 = h * HEAD_DIM
        q = _norm_rope(zq_ref[:, c0:c0 + HEAD_DIM], gq, cos, sin).astype(BF16)
        s = lax.dot_general(q, k, (((1,), (1,)), ((), ())), preferred_element_type=F32)
        mx = jnp.max(s, axis=-1, keepdims=True)
        p = jnp.exp2((s - mx) * c).astype(BF16)
        o = jnp.dot(p, v, preferred_element_type=F32)
        o_ref[:, c0:c0 + HEAD_DIM] = (o[:, 0:HEAD_DIM] / o[:, HEAD_DIM:2 * HEAD_DIM]).astype(o_ref.dtype)


def attention(z, gq3, gk3, cos_t, sin_t, layer, batch, seq, name, tq=256):
    m = z.shape[0]
    tq = min(tq, seq)
    nq = seq // tq
    gw = Q_PER_KV * HEAD_DIM
    kcol = ATTN_WIDTH // HEAD_DIM
    vcol = (ATTN_WIDTH + KV_WIDTH) // HEAD_DIM
    gspec = pl.BlockSpec((None, 1, HEAD_DIM), lambda b, g, i: (layer, 0, 0))
    tspec = pl.BlockSpec((seq, HEAD_DIM), lambda b, g, i: (0, 0))
    kern = functools.partial(_attn_kernel, tq=tq, seq=seq)
    return pl.pallas_call(
        kern,
        out_shape=jax.ShapeDtypeStruct((m, ATTN_WIDTH), F32),
        grid=(batch, N_KV_HEADS, nq),
        in_specs=[
            pl.BlockSpec((tq, gw), lambda b, g, i: (b * nq + i, g)),
            pl.BlockSpec((seq, HEAD_DIM), lambda b, g, i: (b, kcol + g)),
            pl.BlockSpec((seq, HEAD_DIM), lambda b, g, i: (b, vcol + g)),
            gspec, gspec, tspec, tspec,
        ],
        out_specs=pl.BlockSpec((tq, gw), lambda b, g, i: (b * nq + i, g)),
        scratch_shapes=[pltpu.VMEM((seq, HEAD_DIM), BF16), pltpu.VMEM((seq, 2 * HEAD_DIM), BF16)],
        compiler_params=_cparams(("arbitrary", "arbitrary", "arbitrary")),
        name=name,
    )(z, z, z, gq3, gk3, cos_t, sin_t)


def _xattn_kernel(q_ref, k_ref, v_ref, o_ref, *, head_dim):
    scale = head_dim ** -0.5
    for h in range(MEM_HEADS):
        c0 = h * head_dim
        q = q_ref[:, c0:c0 + head_dim]
        k = k_ref[:, c0:c0 + head_dim]
        v = v_ref[:, c0:c0 + head_dim]
        s = lax.dot_general(q, k, (((1,), (1,)), ((), ())), preferred_element_type=F32) * scale
        mx = jnp.max(s, axis=-1, keepdims=True)
        p = jnp.exp(s - mx)
        den = jnp.sum(p, axis=-1, keepdims=True)
        o = jnp.dot(p.astype(v.dtype), v, preferred_element_type=F32)
        o_ref[:, c0:c0 + head_dim] = (o / den).astype(o_ref.dtype)


def cross_attention(q, k, v, batch, seq, n_mem, name, tq=512):
    m, d = q.shape
    tq = min(tq, seq)
    nq = seq // tq
    kern = functools.partial(_xattn_kernel, head_dim=d // MEM_HEADS)
    return pl.pallas_call(
        kern,
        out_shape=jax.ShapeDtypeStruct((m, d), BF16),
        grid=(batch, nq),
        in_specs=[
            pl.BlockSpec((tq, d), lambda b, i: (b * nq + i, 0)),
            pl.BlockSpec((n_mem, d), lambda b, i: (b, 0)),
            pl.BlockSpec((n_mem, d), lambda b, i: (b, 0)),
        ],
        out_specs=pl.BlockSpec((tq, d), lambda b, i: (b * nq + i, 0)),
        compiler_params=_cparams(("arbitrary", "arbitrary")),
        name=name,
    )(q, k, v)


RG_PAD = V7X_SUBLANES


def _rglru_kernel(x_ref, gate_ref, cw_ref, cb_ref, w_ref, ba_ref, bx_ref, lam_ref, y_ref,
                  pad_ref, af_ref, uf_ref, ab_ref, ub_ref, *, seq, heads):
    cw = heads * RG_BLOCK
    zeros = jnp.zeros((RG_PAD, cw), F32)
    pad_ref[0:RG_PAD, :] = zeros
    pad_ref[RG_PAD + seq:RG_PAD + seq + RG_PAD, :] = zeros
    pad_ref[RG_PAD:RG_PAD + seq, :] = x_ref[...]
    left = RG_CONV_W // 2
    xc = cb_ref[...] + cw_ref[0:1, :] * pad_ref[pl.ds(RG_PAD - left, seq), :]
    for j in range(1, RG_CONV_W):
        xc = xc + cw_ref[j:j + 1, :] * pad_ref[pl.ds(RG_PAD - left + j, seq), :]
    xcb = xc.astype(BF16)

    lam = lam_ref[...]
    nl = -lam
    softplus = jnp.maximum(nl, 0.0) + jnp.log1p(jnp.exp(-jnp.abs(nl)))
    ba = ba_ref[...]
    bx = bx_ref[...]
    for h in range(heads):
        c0 = h * RG_BLOCK
        sl = slice(c0, c0 + RG_BLOCK)
        gates = jnp.dot(xcb[:, sl], w_ref[h].astype(BF16), preferred_element_type=F32)
        xch = xc[:, sl]
        for d, (a_ref, u_ref) in enumerate(((af_ref, uf_ref), (ab_ref, ub_ref))):
            g0 = 2 * d * RG_BLOCK
            r = _sigmoid(gates[:, g0:g0 + RG_BLOCK] + ba[d:d + 1, sl])
            i = _sigmoid(gates[:, g0 + RG_BLOCK:g0 + 2 * RG_BLOCK] + bx[d:d + 1, sl])
            log_a = (-RG_C) * r * softplus[d:d + 1, sl]
            t = jnp.tanh(log_a)
            one_minus_a2 = (-2.0 * t) / (1.0 - t)
            a_ref[:, sl] = jnp.exp(log_a)
            u_ref[:, sl] = jnp.sqrt(one_minus_a2) * (i * xch)

    n_tiles = seq // V7X_SUBLANES
    row = lax.broadcasted_iota(jnp.int32, (V7X_SUBLANES, cw), 0)

    def tile_scan(a, u, reverse):
        for d in (1, 2, 4):
            if reverse:
                shift, keep = V7X_SUBLANES - d, row < V7X_SUBLANES - d
            else:
                shift, keep = d, row >= d
            a_s = pltpu.roll(a, shift, axis=0)
            u_s = pltpu.roll(u, shift, axis=0)
            u = jnp.where(keep, u + a * u_s, u)
            a = jnp.where(keep, a * a_s, a)
        return a, u

    def body(t, carry):
        hf, hb = carry
        rf = pl.multiple_of(t * V7X_SUBLANES, V7X_SUBLANES)
        a, u = tile_scan(af_ref[pl.ds(rf, V7X_SUBLANES), :], uf_ref[pl.ds(rf, V7X_SUBLANES), :], False)
        hh = a * hf + u
        uf_ref[pl.ds(rf, V7X_SUBLANES), :] = hh
        hf = jnp.broadcast_to(hh[V7X_SUBLANES - 1:V7X_SUBLANES, :], hh.shape)
        rb = pl.multiple_of((n_tiles - 1 - t) * V7X_SUBLANES, V7X_SUBLANES)
        a, u = tile_scan(ab_ref[pl.ds(rb, V7X_SUBLANES), :], ub_ref[pl.ds(rb, V7X_SUBLANES), :], True)
        hh = a * hb + u
        ub_ref[pl.ds(rb, V7X_SUBLANES), :] = hh
        hb = jnp.broadcast_to(hh[0:1, :], hh.shape)
        return hf, hb

    h0 = jnp.zeros((V7X_SUBLANES, cw), F32)
    lax.fori_loop(0, n_tiles, body, (h0, h0))

    g = gate_ref[...]
    gelu = 0.5 * g * (1.0 + jnp.tanh(math.sqrt(2.0 / math.pi) * (g + 0.044715 * (g * g * g))))
    y_ref[...] = ((uf_ref[...] + ub_ref[...]) * gelu).astype(y_ref.dtype)


def rglru(z3, cw3, cb3, w4, ba3, bx3, lam3, layer, name, heads=2):
    b, seq, _ = z3.shape
    cw = heads * RG_BLOCK
    ncb = RG_WIDTH // cw
    n_rg_heads = RG_WIDTH // RG_BLOCK
    kern = functools.partial(_rglru_kernel, seq=seq, heads=heads)
    vec = lambda rows: pl.BlockSpec((None, rows, cw), lambda bi, j: (layer, 0, j))
    return pl.pallas_call(
        kern,
        out_shape=jax.ShapeDtypeStruct((b, seq, RG_WIDTH), F32),
        grid=(b, ncb),
        in_specs=[
            pl.BlockSpec((None, seq, cw), lambda bi, j: (bi, 0, COL_RX * ncb + j)),
            pl.BlockSpec((None, seq, cw), lambda bi, j: (bi, 0, COL_RG * ncb + j)),
            vec(RG_CONV_W), vec(1),
            pl.BlockSpec((heads, RG_BLOCK, 4 * RG_BLOCK), lambda bi, j: ((layer * n_rg_heads) // heads + j, 0, 0)),
            vec(2), vec(2), vec(2),
        ],
        out_specs=pl.BlockSpec((None, seq, cw), lambda bi, j: (bi, 0, j)),
        scratch_shapes=[pltpu.VMEM((seq + 2 * RG_PAD, cw), F32)] + [pltpu.VMEM((seq, cw), F32)] * 4,
        compiler_params=_cparams(("arbitrary", "arbitrary")),
        name=name,
    )(z3, z3, cw3, cb3, w4, ba3, bx3, lam3)


CV_CHUNK = 32


def _conformer_kernel(a_ref, g_ref, ap_ref, gp_ref, an_ref, gn_ref, dw_ref, db_ref, lg_ref, lb_ref,
                      pw_ref, pb_ref, gg_ref, o_ref, buf_ref, sh_ref, conv_ref, pwb_ref, dwb_ref, *, ts):
    i = pl.program_id(1)
    n_i = pl.num_programs(1)

    @pl.when(jnp.logical_and(pl.program_id(0) == 0, i == 0))
    def _():
        pwb_ref[...] = pw_ref[...].astype(BF16)
        for j in range(CV_KERNEL):
            dwb_ref[j] = jnp.broadcast_to(dw_ref[j:j + 1, :], (V7X_SUBLANES, CV_WIDTH))

    def glu(a, g):
        return a * _sigmoid(g)

    prev = glu(ap_ref[...], gp_ref[...])
    nxt = glu(an_ref[...], gn_ref[...])
    buf_ref[0:CV_HALO, :] = jnp.where(i > 0, prev, 0.0)
    buf_ref[CV_HALO:CV_HALO + ts, :] = glu(a_ref[...], g_ref[...])
    buf_ref[CV_HALO + ts:CV_HALO + ts + CV_HALO, :] = jnp.where(i < n_i - 1, nxt, 0.0)

    span = ts + 2 * CV_HALO - V7X_SUBLANES
    for r in range(V7X_SUBLANES):
        sh_ref[r] = buf_ref[pl.ds(r, span), :]

    base_off = CV_HALO - CV_KERNEL // 2
    n_sub = CV_CHUNK // V7X_SUBLANES
    bias = jnp.broadcast_to(db_ref[...], (V7X_SUBLANES, CV_WIDTH))

    def chunk(c, carry):
        r0 = pl.multiple_of(c * CV_CHUNK, CV_CHUNK)
        accs = [bias] * n_sub
        for j in range(CV_KERNEL):
            off = base_off + j
            q, r = off // V7X_SUBLANES, off % V7X_SUBLANES
            w = dwb_ref[j]
            for s in range(n_sub):
                rows = pl.multiple_of(r0 + (q + s) * V7X_SUBLANES, V7X_SUBLANES)
                accs[s] = accs[s] + w * sh_ref[r, pl.ds(rows, V7X_SUBLANES), :]
        for s in range(n_sub):
            rows = pl.multiple_of(r0 + s * V7X_SUBLANES, V7X_SUBLANES)
            conv_ref[pl.ds(rows, V7X_SUBLANES), :] = accs[s]
        return carry

    lax.fori_loop(0, ts // CV_CHUNK, chunk, 0)

    y = conv_ref[...]
    mu = jnp.mean(y, axis=-1, keepdims=True)
    yc = y - mu
    var = jnp.mean(yc * yc, axis=-1, keepdims=True)
    yn = yc * lax.rsqrt(var + EPS) * lg_ref[...] + lb_ref[...]
    act = yn * _sigmoid(yn)
    out = jnp.dot(act.astype(BF16), pwb_ref[...], preferred_element_type=F32) + pb_ref[...]
    ms = jnp.mean(out * out, axis=-1, keepdims=True)
    o_ref[...] = (out * lax.rsqrt(ms + EPS) * gg_ref[...]).astype(o_ref.dtype)


def conformer(z3, dw3, db3, lg3, lb3, pw3, pb3, gg3, layer, name, ts=256):
    b, seq, _ = z3.shape
    ts = min(ts, seq)
    nt = seq // ts
    hb = ts // CV_HALO
    n_hblk = seq // CV_HALO
    main = lambda col: pl.BlockSpec((None, ts, CV_WIDTH), lambda bi, i: (bi, i, col))
    prev = lambda col: pl.BlockSpec((None, CV_HALO, CV_WIDTH),
                                    lambda bi, i: (bi, jnp.maximum(i * hb - 1, 0), col))
    nxt = lambda col: pl.BlockSpec((None, CV_HALO, CV_WIDTH),
                                   lambda bi, i: (bi, jnp.minimum((i + 1) * hb, n_hblk - 1), col))
    vec = lambda rows: pl.BlockSpec((None, rows, CV_WIDTH), lambda bi, i: (layer, 0, 0))
    kern = functools.partial(_conformer_kernel, ts=ts)
    return pl.pallas_call(
        kern,
        out_shape=jax.ShapeDtypeStruct((b, seq, CV_WIDTH), BF16),
        grid=(b, nt),
        in_specs=[
            main(COL_CA), main(COL_CB), prev(COL_CA), prev(COL_CB), nxt(COL_CA), nxt(COL_CB),
            vec(CV_KERNEL), vec(1), vec(1), vec(1),
            pl.BlockSpec((None, CV_WIDTH, CV_WIDTH), lambda bi, i: (layer, 0, 0)),
            vec(1),
            pl.BlockSpec((None, 1, CV_WIDTH), lambda bi, i: (layer, 0, (ATTN_WIDTH + RG_WIDTH) // CV_WIDTH)),
        ],
        out_specs=pl.BlockSpec((None, ts, CV_WIDTH), lambda bi, i: (bi, i, 0)),
        scratch_shapes=[
            pltpu.VMEM((ts + 2 * CV_HALO, CV_WIDTH), F32),
            pltpu.VMEM((V7X_SUBLANES, ts + 2 * CV_HALO - V7X_SUBLANES, CV_WIDTH), F32),
            pltpu.VMEM((ts, CV_WIDTH), F32),
            pltpu.VMEM((CV_WIDTH, CV_WIDTH), BF16),
            pltpu.VMEM((CV_KERNEL, V7X_SUBLANES, CV_WIDTH), F32),
        ],
        compiler_params=_cparams(("arbitrary", "arbitrary")),
        name=name,
    )(z3, z3, z3, z3, z3, z3, dw3, db3, lg3, lb3, pw3, pb3, gg3)


def kernel(x, mem, g_mix, w_in, g_q, g_k, rg_conv_w, rg_conv_b, rg_w_a, rg_b_a, rg_w_x, rg_b_x, rg_lam, cv_dw_w, cv_dw_b, cv_ln_g, cv_ln_b, cv_pw_w, cv_pw_b, g_grp, w_out, g_xattn, g_mem, xa_wq, xa_wk, xa_wv, xa_wo, g_ffn, ffn_wg, ffn_wu, ffn_wd, g_final):
    b, seq, d = x.shape
    n_mem = mem.shape[1]
    depth = w_in.shape[0]
    m = b * seq
    row3 = lambda p: p.reshape(p.shape[0], 1, p.shape[-1])

    cos_t, sin_t = rope_tables(seq)
    g_mix3, g_q3, g_k3, g_grp3 = row3(g_mix), row3(g_q), row3(g_k), row3(g_grp)
    g_xattn3, g_mem3, g_ffn3 = row3(g_xattn), row3(g_mem), row3(g_ffn)
    g_final3 = g_final.reshape(1, 1, d)
    rg_cb3, cv_db3, cv_lg3, cv_lb3, cv_pb3 = (row3(rg_conv_b), row3(cv_dw_b), row3(cv_ln_g),
                                              row3(cv_ln_b), row3(cv_pw_b))
    rg_w4 = jnp.concatenate([rg_w_a[:, 0], rg_w_x[:, 0], rg_w_a[:, 1], rg_w_x[:, 1]], axis=-1)
    rg_w4 = rg_w4.reshape(-1, RG_BLOCK, 4 * RG_BLOCK)

    xf = x.reshape(m, d)
    memf = mem.reshape(b * n_mem, d)

    for l in range(depth):
        h = rmsnorm(xf, g_mix3, l, 0, BF16, f"norm_mix{l}")
        z = matmul_stream([h], w_in, l, F32, f"in_proj{l}", tn=1024, tm=1024)
        z3 = z.reshape(b, seq, z.shape[1])
        y_attn = attention(z, g_q3, g_k3, cos_t, sin_t, l, b, seq, f"attn{l}")
        y_rec = rglru(z3, rg_conv_w, rg_cb3, rg_w4, rg_b_a, rg_b_x, rg_lam, l, f"rglru{l}")
        y_cv_n = conformer(z3, cv_dw_w, cv_db3, cv_lg3, cv_lb3, cv_pw_w, cv_pb3, g_grp3, l, f"conformer{l}")
        y_attn_n = rmsnorm(y_attn, g_grp3, l, 0, BF16, f"norm_attn{l}")
        y_rec_n = rmsnorm(y_rec.reshape(m, RG_WIDTH), g_grp3, l, ATTN_WIDTH // RG_WIDTH, BF16, f"norm_rec{l}")
        xf = matmul_stream([y_attn_n, y_rec_n, y_cv_n.reshape(m, CV_WIDTH)], w_out, l, F32, f"out_proj{l}",
                           tn=1024, tm=1024, res=xf)

        h = rmsnorm(xf, g_xattn3, l, 0, BF16, f"norm_xattn{l}")
        mn = rmsnorm(memf, g_mem3, l, 0, BF16, f"norm_mem{l}")
        q2 = matmul_stream([h], xa_wq, l, BF16, f"xa_q{l}", tn=1024, tm=1024)
        k2 = matmul(mn, xa_wk, l, BF16, f"xa_k{l}", tn=512, tm=1024)
        v2 = matmul(mn, xa_wv, l, BF16, f"xa_v{l}", tn=512, tm=1024)
        o2 = cross_attention(q2, k2, v2, b, seq, n_mem, f"xattn{l}")
        xf = matmul_stream([o2], xa_wo, l, F32, f"xa_o{l}", tn=1024, tm=1024, res=xf)

        h = rmsnorm(xf, g_ffn3, l, 0, BF16, f"norm_ffn{l}")
        act = swiglu_stream(h, ffn_wg, ffn_wu, l, f"ffn_up{l}", tn=512, tm=1024)
        xf = matmul_stream([act], ffn_wd, l, F32, f"ffn_down{l}", tn=512, tm=512, res=xf)

    out = rmsnorm(xf, g_final3, 0, 0, F32, "norm_final")
    return out.reshape(b, seq, d)
```

```python
import functools
import math

import jax
import jax.numpy as jnp
from jax import lax
from jax.experimental import pallas as pl
from jax.experimental.pallas import tpu as pltpu

F32 = jnp.float32
BF16 = jnp.bfloat16

HEAD_DIM = 128
N_Q_HEADS = 16
N_KV_HEADS = 4
Q_PER_KV = N_Q_HEADS // N_KV_HEADS
ATTN_WIDTH = N_Q_HEADS * HEAD_DIM
KV_WIDTH = N_KV_HEADS * HEAD_DIM
RG_WIDTH = 1024
RG_BLOCK = 128
RG_CONV_W = 4
RG_C = 8.0
CV_WIDTH = 1024
CV_KERNEL = 31
QKV_WIDTH = ATTN_WIDTH + 2 * KV_WIDTH
COL_RX = QKV_WIDTH // RG_WIDTH
COL_RG = COL_RX + 1
COL_CA = COL_RG + 1
COL_CB = COL_CA + 1
MEM_HEADS = 4
GRID_W = 64
ROPE_THETA = 10000.0
ROPE_AXIS_DIM = HEAD_DIM // 2
EPS = 1e-6

V7X_VMEM_BYTES = 64 * 1024 * 1024
V7X_SUBLANES = 8
V7X_LANES = 128
VMEM_LIMIT = V7X_VMEM_BYTES - 4 * 1024 * 1024

CV_HALO = 16


def _sigmoid(x):
    return 0.5 * jnp.tanh(0.5 * x) + 0.5


def _cparams(sem):
    return pltpu.CompilerParams(dimension_semantics=sem, vmem_limit_bytes=VMEM_LIMIT)


def _rmsnorm_kernel(x_ref, g_ref, o_ref):
    x = x_ref[...].astype(F32)
    ms = jnp.mean(x * x, axis=-1, keepdims=True)
    o_ref[...] = (x * lax.rsqrt(ms + EPS) * g_ref[...]).astype(o_ref.dtype)


def rmsnorm(x, g3, layer, gcol, out_dtype, name, tm=256):
    m, w = x.shape
    tm = min(tm, m)
    return pl.pallas_call(
        _rmsnorm_kernel,
        out_shape=jax.ShapeDtypeStruct((m, w), out_dtype),
        grid=(m // tm,),
        in_specs=[
            pl.BlockSpec((tm, w), lambda i: (i, 0)),
            pl.BlockSpec((None, 1, w), lambda i: (layer, 0, gcol)),
        ],
        out_specs=pl.BlockSpec((tm, w), lambda i: (i, 0)),
        compiler_params=_cparams(("arbitrary",)),
        name=name,
    )(x, g3)


CAST_ROWS = 256


def _cast_panel(w_ref, wb_ref, g_ref=None):
    k, tn = w_ref.shape

    def body(i, c):
        rows = pl.ds(pl.multiple_of(i * CAST_ROWS, CAST_ROWS), CAST_ROWS)
        if g_ref is None:
            wb_ref[rows, :] = w_ref[rows, :].astype(BF16)
        else:
            g = g_ref[rows, :]
            for c0 in range(0, tn, V7X_LANES):
                wb_ref[rows, c0:c0 + V7X_LANES] = (w_ref[rows, c0:c0 + V7X_LANES] * g).astype(BF16)
        return c

    lax.fori_loop(0, k // CAST_ROWS, body, 0)


STAT_LANES = V7X_LANES


def _mm_kernel(*refs, k_splits, has_norm_in, has_res, emit_norm, w_is_bf16, n_total):
    n_a = len(k_splits)
    a_refs = refs[:n_a]
    w_ref = refs[n_a]
    pos = n_a + 1
    g_ref = rin_ref = res_ref = xb_ref = rout_ref = ssq_ref = None
    if has_norm_in:
        g_ref, rin_ref = refs[pos], refs[pos + 1]
        pos += 2
    if has_res:
        res_ref = refs[pos]
        pos += 1
    o_ref = refs[pos]
    pos += 1
    if emit_norm:
        xb_ref, rout_ref = refs[pos], refs[pos + 1]
        pos += 2
    j = pl.program_id(0)
    i = pl.program_id(1)
    if w_is_bf16:
        wb_ref = w_ref
    else:
        wb_ref = refs[pos]
        pos += 1

        @pl.when(i == 0)
        def _():
            _cast_panel(w_ref, wb_ref, g_ref)

    acc = None
    off = 0
    for a_ref, kk in zip(a_refs, k_splits):
        d = jnp.dot(a_ref[...], wb_ref[off:off + kk, :], preferred_element_type=F32)
        acc = d if acc is None else acc + d
        off += kk
    if has_norm_in:
        acc = acc * rin_ref[:, 0:1]
    if has_res:
        acc = acc + res_ref[...]
    o_ref[...] = acc.astype(o_ref.dtype)
    if emit_norm:
        ssq_ref = refs[pos]
        tm = acc.shape[0]
        xb_ref[...] = acc.astype(BF16)
        rows = pl.ds(pl.multiple_of(i * tm, tm), tm)
        part = jnp.broadcast_to(jnp.sum(acc * acc, axis=-1, keepdims=True), (tm, STAT_LANES))

        @pl.when(j == 0)
        def _():
            ssq_ref[rows, :] = part

        @pl.when(j > 0)
        def _():
            ssq_ref[rows, :] = ssq_ref[rows, :] + part

        rout_ref[...] = lax.rsqrt(ssq_ref[rows, :] * (1.0 / n_total) + EPS)


def matmul(a_list, w3, layer, out_dtype, name, *, tn, tm, norm_in=None, res=None, emit_norm=False):
    m = a_list[0].shape[0]
    k_splits = tuple(a.shape[1] for a in a_list)
    _, k, n = w3.shape
    assert sum(k_splits) == k
    tm = min(tm, m)
    w_is_bf16 = w3.dtype == BF16
    assert not (w_is_bf16 and norm_in is not None)
    in_specs = [pl.BlockSpec((tm, kk), lambda j, i: (i, 0)) for kk in k_splits]
    in_specs.append(pl.BlockSpec((None, k, tn), lambda j, i: (layer, 0, j)))
    args = list(a_list) + [w3]
    if norm_in is not None:
        last_slab = norm_in[1].shape[0] - 1
        in_specs.append(pl.BlockSpec((None, k, V7X_LANES), lambda j, i: (layer, 0, 0)))
        in_specs.append(pl.BlockSpec((None, tm, STAT_LANES), lambda j, i: (last_slab, i, 0)))
        args += list(norm_in)
    if res is not None:
        in_specs.append(pl.BlockSpec((tm, tn), lambda j, i: (i, j)))
        args.append(res)
    out_shape = [jax.ShapeDtypeStruct((m, n), out_dtype)]
    out_specs = [pl.BlockSpec((tm, tn), lambda j, i: (i, j))]
    scratch = [] if w_is_bf16 else [pltpu.VMEM((k, tn), BF16)]
    if emit_norm:
        out_shape += [jax.ShapeDtypeStruct((m, n), BF16), jax.ShapeDtypeStruct((n // tn, m, STAT_LANES), F32)]
        out_specs += [pl.BlockSpec((tm, tn), lambda j, i: (i, j)),
                      pl.BlockSpec((None, tm, STAT_LANES), lambda j, i: (j, i, 0))]
        scratch.append(pltpu.VMEM((m, STAT_LANES), F32))
    kern = functools.partial(_mm_kernel, k_splits=k_splits, has_norm_in=norm_in is not None,
                             has_res=res is not None, emit_norm=emit_norm, w_is_bf16=w_is_bf16, n_total=n)
    out = pl.pallas_call(
        kern,
        out_shape=tuple(out_shape),
        grid=(n // tn, m // tm),
        in_specs=in_specs,
        out_specs=tuple(out_specs),
        scratch_shapes=scratch,
        compiler_params=_cparams(("arbitrary", "arbitrary")),
        name=name,
    )(*args)
    return out if emit_norm else out[0]


def _row_stats_kernel(x_ref, xb_ref, r_ref):
    x = x_ref[...]
    xb_ref[...] = x.astype(BF16)
    ms = jnp.mean(x * x, axis=-1, keepdims=True)
    r_ref[...] = jnp.broadcast_to(lax.rsqrt(ms + EPS), r_ref.shape)


def row_stats(x, name, tm=256):
    m, w = x.shape
    tm = min(tm, m)
    return pl.pallas_call(
        _row_stats_kernel,
        out_shape=(jax.ShapeDtypeStruct((m, w), BF16), jax.ShapeDtypeStruct((1, m, STAT_LANES), F32)),
        grid=(m // tm,),
        in_specs=[pl.BlockSpec((tm, w), lambda i: (i, 0))],
        out_specs=(pl.BlockSpec((tm, w), lambda i: (i, 0)),
                   pl.BlockSpec((None, tm, STAT_LANES), lambda i: (0, i, 0))),
        compiler_params=_cparams(("arbitrary",)),
        name=name,
    )(x)


def _swiglu_kernel(a_ref, wg_ref, wu_ref, g_ref, rin_ref, o_ref, wgb_ref, wub_ref):
    @pl.when(pl.program_id(1) == 0)
    def _():
        _cast_panel(wg_ref, wgb_ref, g_ref)
        _cast_panel(wu_ref, wub_ref, g_ref)

    a = a_ref[...]
    r = rin_ref[:, 0:1]
    g = jnp.dot(a, wgb_ref[...], preferred_element_type=F32) * r
    u = jnp.dot(a, wub_ref[...], preferred_element_type=F32) * r
    o_ref[...] = (g * _sigmoid(g) * u).astype(o_ref.dtype)


def swiglu(a, wg3, wu3, g3, rin, layer, name, *, tn, tm):
    m, k = a.shape
    n = wg3.shape[2]
    tm = min(tm, m)
    assert n % tn == 0
    wspec = pl.BlockSpec((None, k, tn), lambda j, i: (layer, 0, j))
    last_slab = rin.shape[0] - 1
    return pl.pallas_call(
        _swiglu_kernel,
        out_shape=jax.ShapeDtypeStruct((m, n), BF16),
        grid=(n // tn, m // tm),
        in_specs=[pl.BlockSpec((tm, k), lambda j, i: (i, 0)), wspec, wspec,
                  pl.BlockSpec((None, k, V7X_LANES), lambda j, i: (layer, 0, 0)),
                  pl.BlockSpec((None, tm, STAT_LANES), lambda j, i: (last_slab, i, 0))],
        out_specs=pl.BlockSpec((tm, tn), lambda j, i: (i, j)),
        scratch_shapes=[pltpu.VMEM((k, tn), BF16), pltpu.VMEM((k, tn), BF16)],
        compiler_params=_cparams(("arbitrary", "arbitrary")),
        name=name,
    )(a, wg3, wu3, g3, rin)


def _cast_kernel(x_ref, o_ref):
    o_ref[...] = x_ref[...].astype(o_ref.dtype)


def cast_bf16(w3, name, rows=256):
    l, k, n = w3.shape
    return pl.pallas_call(
        _cast_kernel,
        out_shape=jax.ShapeDtypeStruct((l, k, n), BF16),
        grid=(l, k // rows),
        in_specs=[pl.BlockSpec((None, rows, n), lambda a, i: (a, i, 0))],
        out_specs=pl.BlockSpec((None, rows, n), lambda a, i: (a, i, 0)),
        compiler_params=_cparams(("arbitrary", "arbitrary")),
        name=name,
    )(w3)


def rope_tables(seq):
    rows = seq // GRID_W
    row = jnp.repeat(jnp.arange(rows), GRID_W).astype(F32)
    col = jnp.tile(jnp.arange(GRID_W), rows).astype(F32)
    inv = ROPE_THETA ** (-jnp.arange(0, ROPE_AXIS_DIM, 2, dtype=F32) / ROPE_AXIS_DIM)
    ang_r = row[:, None] * inv
    ang_c = col[:, None] * inv
    cos_t = jnp.concatenate([jnp.cos(ang_r), jnp.cos(ang_r), jnp.cos(ang_c), jnp.cos(ang_c)], axis=1)
    sin_t = jnp.concatenate([-jnp.sin(ang_r), jnp.sin(ang_r), -jnp.sin(ang_c), jnp.sin(ang_c)], axis=1)
    return cos_t, sin_t


def _norm_rope(x, g, cos, sin):
    half = ROPE_AXIS_DIM // 2
    lane = lax.broadcasted_iota(jnp.int32, (1, HEAD_DIM), 1)
    first_half = (lane % ROPE_AXIS_DIM) < half
    ms = jnp.mean(x * x, axis=-1, keepdims=True)
    xn = x * lax.rsqrt(ms + EPS) * g
    swapped = jnp.where(first_half, pltpu.roll(xn, HEAD_DIM - half, axis=1), pltpu.roll(xn, half, axis=1))
    return xn * cos + swapped * sin


def _attn_kernel(zq_ref, zk_ref, zv_ref, gq_ref, gk_ref, cos_ref, sin_ref, o_ref, k_s, v_s, *, tq, seq):
    i = pl.program_id(2)

    @pl.when(i == 0)
    def _():
        k_s[...] = _norm_rope(zk_ref[...], gk_ref[...], cos_ref[...], sin_ref[...]).astype(BF16)
        v_s[:, 0:HEAD_DIM] = zv_ref[...].astype(BF16)
        v_s[:, HEAD_DIM:2 * HEAD_DIM] = jnp.ones((seq, HEAD_DIM), BF16)

    r0 = pl.multiple_of(i * tq, tq)
    cos = cos_ref[pl.ds(r0, tq), :]
    sin = sin_ref[pl.ds(r0, tq), :]
    gq = gq_ref[...]
    k = k_s[...]
    v = v_s[...]
    c = (HEAD_DIM ** -0.5) * math.log2(math.e)
    for h in range(Q_PER_KV):
        c0 = h * HEAD_DIM
        q = _norm_rope(zq_ref[:, c0:c0 + HEAD_DIM], gq, cos, sin).astype(BF16)
        s = lax.dot_general(q, k, (((1,), (1,)), ((), ())), preferred_element_type=F32)
        mx = jnp.max(s, axis=-1, keepdims=True)
        p = jnp.exp2((s - mx) * c).astype(BF16)
        o = jnp.dot(p, v, preferred_element_type=F32)
        o_ref[:, c0:c0 + HEAD_DIM] = (o[:, 0:HEAD_DIM] / o[:, HEAD_DIM:2 * HEAD_DIM]).astype(o_ref.dtype)


def attention(z, gq3, gk3, cos_t, sin_t, layer, batch, seq, name, tq=256):
    m = z.shape[0]
    tq = min(tq, seq)
    nq = seq // tq
    gw = Q_PER_KV * HEAD_DIM
    kcol = ATTN_WIDTH // HEAD_DIM
    vcol = (ATTN_WIDTH + KV_WIDTH) // HEAD_DIM
    gspec = pl.BlockSpec((None, 1, HEAD_DIM), lambda b, g, i: (layer, 0, 0))
    tspec = pl.BlockSpec((seq, HEAD_DIM), lambda b, g, i: (0, 0))
    kern = functools.partial(_attn_kernel, tq=tq, seq=seq)
    return pl.pallas_call(
        kern,
        out_shape=jax.ShapeDtypeStruct((m, ATTN_WIDTH), F32),
        grid=(batch, N_KV_HEADS, nq),
        in_specs=[
            pl.BlockSpec((tq, gw), lambda b, g, i: (b * nq + i, g)),
            pl.BlockSpec((seq, HEAD_DIM), lambda b, g, i: (b, kcol + g)),
            pl.BlockSpec((seq, HEAD_DIM), lambda b, g, i: (b, vcol + g)),
            gspec, gspec, tspec, tspec,
        ],
        out_specs=pl.BlockSpec((tq, gw), lambda b, g, i: (b * nq + i, g)),
        scratch_shapes=[pltpu.VMEM((seq, HEAD_DIM), BF16), pltpu.VMEM((seq, 2 * HEAD_DIM), BF16)],
        compiler_params=_cparams(("arbitrary", "arbitrary", "arbitrary")),
        name=name,
    )(z, z, z, gq3, gk3, cos_t, sin_t)


def _xattn_kernel(q_ref, k_ref, v_ref, o_ref, *, head_dim):
    scale = head_dim ** -0.5
    for h in range(MEM_HEADS):
        c0 = h * head_dim
        q = q_ref[:, c0:c0 + head_dim]
        k = k_ref[:, c0:c0 + head_dim]
        v = v_ref[:, c0:c0 + head_dim]
        s = lax.dot_general(q, k, (((1,), (1,)), ((), ())), preferred_element_type=F32) * scale
        mx = jnp.max(s, axis=-1, keepdims=True)
        p = jnp.exp(s - mx)
        den = jnp.sum(p, axis=-1, keepdims=True)
        o = jnp.dot(p.astype(v.dtype), v, preferred_element_type=F32)
        o_ref[:, c0:c0 + head_dim] = (o / den).astype(o_ref.dtype)


def cross_attention(q, k, v, batch, seq, n_mem, name, tq=512):
    m, d = q.shape
    tq = min(tq, seq)
    nq = seq // tq
    kern = functools.partial(_xattn_kernel, head_dim=d // MEM_HEADS)
    return pl.pallas_call(
        kern,
        out_shape=jax.ShapeDtypeStruct((m, d), BF16),
        grid=(batch, nq),
        in_specs=[
            pl.BlockSpec((tq, d), lambda b, i: (b * nq + i, 0)),
            pl.BlockSpec((n_mem, d), lambda b, i: (b, 0)),
            pl.BlockSpec((n_mem, d), lambda b, i: (b, 0)),
        ],
        out_specs=pl.BlockSpec((tq, d), lambda b, i: (b * nq + i, 0)),
        compiler_params=_cparams(("arbitrary", "arbitrary")),
        name=name,
    )(q, k, v)


RG_PAD = V7X_SUBLANES


def _rglru_kernel(x_ref, gate_ref, cw_ref, cb_ref, w_ref, ba_ref, bx_ref, lam_ref, y_ref,
                  pad_ref, af_ref, uf_ref, ab_ref, ub_ref, *, seq, heads):
    cw = heads * RG_BLOCK
    zeros = jnp.zeros((RG_PAD, cw), F32)
    pad_ref[0:RG_PAD, :] = zeros
    pad_ref[RG_PAD + seq:RG_PAD + seq + RG_PAD, :] = zeros
    pad_ref[RG_PAD:RG_PAD + seq, :] = x_ref[...]
    left = RG_CONV_W // 2
    xc = cb_ref[...] + cw_ref[0:1, :] * pad_ref[pl.ds(RG_PAD - left, seq), :]
    for j in range(1, RG_CONV_W):
        xc = xc + cw_ref[j:j + 1, :] * pad_ref[pl.ds(RG_PAD - left + j, seq), :]
    xcb = xc.astype(BF16)

    lam = lam_ref[...]
    nl = -lam
    softplus = jnp.maximum(nl, 0.0) + jnp.log1p(jnp.exp(-jnp.abs(nl)))
    ba = ba_ref[...]
    bx = bx_ref[...]
    for h in range(heads):
        c0 = h * RG_BLOCK
        sl = slice(c0, c0 + RG_BLOCK)
        gates = jnp.dot(xcb[:, sl], w_ref[h].astype(BF16), preferred_element_type=F32)
        xch = xc[:, sl]
        for d, (a_ref, u_ref) in enumerate(((af_ref, uf_ref), (ab_ref, ub_ref))):
            g0 = 2 * d * RG_BLOCK
            r = _sigmoid(gates[:, g0:g0 + RG_BLOCK] + ba[d:d + 1, sl])
            i = _sigmoid(gates[:, g0 + RG_BLOCK:g0 + 2 * RG_BLOCK] + bx[d:d + 1, sl])
            log_a = (-RG_C) * r * softplus[d:d + 1, sl]
            t = jnp.tanh(log_a)
            one_minus_a2 = (-2.0 * t) / (1.0 - t)
            a_ref[:, sl] = jnp.exp(log_a)
            u_ref[:, sl] = jnp.sqrt(one_minus_a2) * (i * xch)

    n_tiles = seq // V7X_SUBLANES
    row = lax.broadcasted_iota(jnp.int32, (V7X_SUBLANES, cw), 0)

    def tile_scan(a, u, reverse):
        for d in (1, 2, 4):
            if reverse:
                shift, keep = V7X_SUBLANES - d, row < V7X_SUBLANES - d
            else:
                shift, keep = d, row >= d
            a_s = pltpu.roll(a, shift, axis=0)
            u_s = pltpu.roll(u, shift, axis=0)
            u = jnp.where(keep, u + a * u_s, u)
            a = jnp.where(keep, a * a_s, a)
        return a, u

    def body(t, carry):
        hf, hb = carry
        rf = pl.multiple_of(t * V7X_SUBLANES, V7X_SUBLANES)
        a, u = tile_scan(af_ref[pl.ds(rf, V7X_SUBLANES), :], uf_ref[pl.ds(rf, V7X_SUBLANES), :], False)
        hh = a * hf + u
        uf_ref[pl.ds(rf, V7X_SUBLANES), :] = hh
        hf = jnp.broadcast_to(hh[V7X_SUBLANES - 1:V7X_SUBLANES, :], hh.shape)
        rb = pl.multiple_of((n_tiles - 1 - t) * V7X_SUBLANES, V7X_SUBLANES)
        a, u = tile_scan(ab_ref[pl.ds(rb, V7X_SUBLANES), :], ub_ref[pl.ds(rb, V7X_SUBLANES), :], True)
        hh = a * hb + u
        ub_ref[pl.ds(rb, V7X_SUBLANES), :] = hh
        hb = jnp.broadcast_to(hh[0:1, :], hh.shape)
        return hf, hb

    h0 = jnp.zeros((V7X_SUBLANES, cw), F32)
    lax.fori_loop(0, n_tiles, body, (h0, h0))

    g = gate_ref[...]
    gelu = 0.5 * g * (1.0 + jnp.tanh(math.sqrt(2.0 / math.pi) * (g + 0.044715 * (g * g * g))))
    y_ref[...] = ((uf_ref[...] + ub_ref[...]) * gelu).astype(y_ref.dtype)


def rglru(z3, cw3, cb3, w4, ba3, bx3, lam3, layer, name, heads=2):
    b, seq, _ = z3.shape
    cw = heads * RG_BLOCK
    ncb = RG_WIDTH // cw
    n_rg_heads = RG_WIDTH // RG_BLOCK
    kern = functools.partial(_rglru_kernel, seq=seq, heads=heads)
    vec = lambda rows: pl.BlockSpec((None, rows, cw), lambda bi, j: (layer, 0, j))
    return pl.pallas_call(
        kern,
        out_shape=jax.ShapeDtypeStruct((b, seq, RG_WIDTH), F32),
        grid=(b, ncb),
        in_specs=[
            pl.BlockSpec((None, seq, cw), lambda bi, j: (bi, 0, COL_RX * ncb + j)),
            pl.BlockSpec((None, seq, cw), lambda bi, j: (bi, 0, COL_RG * ncb + j)),
            vec(RG_CONV_W), vec(1),
            pl.BlockSpec((heads, RG_BLOCK, 4 * RG_BLOCK), lambda bi, j: ((layer * n_rg_heads) // heads + j, 0, 0)),
            vec(2), vec(2), vec(2),
        ],
        out_specs=pl.BlockSpec((None, seq, cw), lambda bi, j: (bi, 0, j)),
        scratch_shapes=[pltpu.VMEM((seq + 2 * RG_PAD, cw), F32)] + [pltpu.VMEM((seq, cw), F32)] * 4,
        compiler_params=_cparams(("arbitrary", "arbitrary")),
        name=name,
    )(z3, z3, cw3, cb3, w4, ba3, bx3, lam3)


CV_CHUNK = 32


def _conformer_kernel(a_ref, g_ref, ap_ref, gp_ref, an_ref, gn_ref, dw_ref, db_ref, lg_ref, lb_ref,
                      pw_ref, pb_ref, gg_ref, o_ref, buf_ref, sh_ref, conv_ref, pwb_ref, dwb_ref, *, ts):
    i = pl.program_id(1)
    n_i = pl.num_programs(1)

    @pl.when(jnp.logical_and(pl.program_id(0) == 0, i == 0))
    def _():
        pwb_ref[...] = pw_ref[...].astype(BF16)
        for j in range(CV_KERNEL):
            dwb_ref[j] = jnp.broadcast_to(dw_ref[j:j + 1, :], (V7X_SUBLANES, CV_WIDTH))

    def glu(a, g):
        return a * _sigmoid(g)

    prev = glu(ap_ref[...], gp_ref[...])
    nxt = glu(an_ref[...], gn_ref[...])
    buf_ref[0:CV_HALO, :] = jnp.where(i > 0, prev, 0.0)
    buf_ref[CV_HALO:CV_HALO + ts, :] = glu(a_ref[...], g_ref[...])
    buf_ref[CV_HALO + ts:CV_HALO + ts + CV_HALO, :] = jnp.where(i < n_i - 1, nxt, 0.0)

    span = ts + 2 * CV_HALO - V7X_SUBLANES
    for r in range(V7X_SUBLANES):
        sh_ref[r] = buf_ref[pl.ds(r, span), :]

    base_off = CV_HALO - CV_KERNEL // 2
    n_sub = CV_CHUNK // V7X_SUBLANES
    bias = jnp.broadcast_to(db_ref[...], (V7X_SUBLANES, CV_WIDTH))

    def chunk(c, carry):
        r0 = pl.multiple_of(c * CV_CHUNK, CV_CHUNK)
        accs = [bias] * n_sub
        for j in range(CV_KERNEL):
            off = base_off + j
            q, r = off // V7X_SUBLANES, off % V7X_SUBLANES
            w = dwb_ref[j]
            for s in range(n_sub):
                rows = pl.multiple_of(r0 + (q + s) * V7X_SUBLANES, V7X_SUBLANES)
                accs[s] = accs[s] + w * sh_ref[r, pl.ds(rows, V7X_SUBLANES), :]
        for s in range(n_sub):
            rows = pl.multiple_of(r0 + s * V7X_SUBLANES, V7X_SUBLANES)
            conv_ref[pl.ds(rows, V7X_SUBLANES), :] = accs[s]
        return carry

    lax.fori_loop(0, ts // CV_CHUNK, chunk, 0)

    y = conv_ref[...]
    mu = jnp.mean(y, axis=-1, keepdims=True)
    yc = y - mu
    var = jnp.mean(yc * yc, axis=-1, keepdims=True)
    yn = yc * lax.rsqrt(var + EPS) * lg_ref[...] + lb_ref[...]
    act = yn * _sigmoid(yn)
    out = jnp.dot(act.astype(BF16), pwb_ref[...], preferred_element_type=F32) + pb_ref[...]
    ms = jnp.mean(out * out, axis=-1, keepdims=True)
    o_ref[...] = (out * lax.rsqrt(ms + EPS) * gg_ref[...]).astype(o_ref.dtype)


def conformer(z3, dw3, db3, lg3, lb3, pw3, pb3, gg3, layer, name, ts=256):
    b, seq, _ = z3.shape
    ts = min(ts, seq)
    nt = seq // ts
    hb = ts // CV_HALO
    n_hblk = seq // CV_HALO
    main = lambda col: pl.BlockSpec((None, ts, CV_WIDTH), lambda bi, i: (bi, i, col))
    prev = lambda col: pl.BlockSpec((None, CV_HALO, CV_WIDTH),
                                    lambda bi, i: (bi, jnp.maximum(i * hb - 1, 0), col))
    nxt = lambda col: pl.BlockSpec((None, CV_HALO, CV_WIDTH),
                                   lambda bi, i: (bi, jnp.minimum((i + 1) * hb, n_hblk - 1), col))
    vec = lambda rows: pl.BlockSpec((None, rows, CV_WIDTH), lambda bi, i: (layer, 0, 0))
    kern = functools.partial(_conformer_kernel, ts=ts)
    return pl.pallas_call(
        kern,
        out_shape=jax.ShapeDtypeStruct((b, seq, CV_WIDTH), BF16),
        grid=(b, nt),
        in_specs=[
            main(COL_CA), main(COL_CB), prev(COL_CA), prev(COL_CB), nxt(COL_CA), nxt(COL_CB),
            vec(CV_KERNEL), vec(1), vec(1), vec(1),
            pl.BlockSpec((None, CV_WIDTH, CV_WIDTH), lambda bi, i: (layer, 0, 0)),
            vec(1),
            pl.BlockSpec((None, 1, CV_WIDTH), lambda bi, i: (layer, 0, (ATTN_WIDTH + RG_WIDTH) // CV_WIDTH)),
        ],
        out_specs=pl.BlockSpec((None, ts, CV_WIDTH), lambda bi, i: (bi, i, 0)),
        scratch_shapes=[
            pltpu.VMEM((ts + 2 * CV_HALO, CV_WIDTH), F32),
            pltpu.VMEM((V7X_SUBLANES, ts + 2 * CV_HALO - V7X_SUBLANES, CV_WIDTH), F32),
            pltpu.VMEM((ts, CV_WIDTH), F32),
            pltpu.VMEM((CV_WIDTH, CV_WIDTH), BF16),
            pltpu.VMEM((CV_KERNEL, V7X_SUBLANES, CV_WIDTH), F32),
        ],
        compiler_params=_cparams(("arbitrary", "arbitrary")),
        name=name,
    )(z3, z3, z3, z3, z3, z3, dw3, db3, lg3, lb3, pw3, pb3, gg3)


def kernel(x, mem, g_mix, w_in, g_q, g_k, rg_conv_w, rg_conv_b, rg_w_a, rg_b_a, rg_w_x, rg_b_x, rg_lam, cv_dw_w, cv_dw_b, cv_ln_g, cv_ln_b, cv_pw_w, cv_pw_b, g_grp, w_out, g_xattn, g_mem, xa_wq, xa_wk, xa_wv, xa_wo, g_ffn, ffn_wg, ffn_wu, ffn_wd, g_final):
    b, seq, d = x.shape
    n_mem = mem.shape[1]
    depth = w_in.shape[0]
    m = b * seq
    row3 = lambda p: p.reshape(p.shape[0], 1, p.shape[-1])

    cos_t, sin_t = rope_tables(seq)
    g_q3, g_k3, g_grp3, g_mem3 = row3(g_q), row3(g_k), row3(g_grp), row3(g_mem)
    g_final3 = g_final.reshape(1, 1, d)
    rg_cb3, cv_db3, cv_lg3, cv_lb3, cv_pb3 = (row3(rg_conv_b), row3(cv_dw_b), row3(cv_ln_g),
                                              row3(cv_ln_b), row3(cv_pw_b))
    rg_w4 = jnp.concatenate([rg_w_a[:, 0], rg_w_x[:, 0], rg_w_a[:, 1], rg_w_x[:, 1]], axis=-1)
    rg_w4 = rg_w4.reshape(-1, RG_BLOCK, 4 * RG_BLOCK)

    col3 = lambda p: jnp.broadcast_to(p[:, :, None], (p.shape[0], p.shape[1], V7X_LANES))
    g_mix_c, g_xattn_c, g_ffn_c = col3(g_mix), col3(g_xattn), col3(g_ffn)
    wd_bf16 = cast_bf16(ffn_wd, "cast_wd")

    xf = x.reshape(m, d)
    memf = mem.reshape(b * n_mem, d)
    xb, xr = row_stats(xf, "x_stats")

    for l in range(depth):
        last = l == depth - 1
        z = matmul([xb], w_in, l, F32, f"in_proj{l}", tn=512, tm=1024, norm_in=(g_mix_c, xr))
        z3 = z.reshape(b, seq, z.shape[1])
        y_attn = attention(z, g_q3, g_k3, cos_t, sin_t, l, b, seq, f"attn{l}")
        y_rec = rglru(z3, rg_conv_w, rg_cb3, rg_w4, rg_b_a, rg_b_x, rg_lam, l, f"rglru{l}")
        y_cv_n = conformer(z3, cv_dw_w, cv_db3, cv_lg3, cv_lb3, cv_pw_w, cv_pb3, g_grp3, l, f"conformer{l}")
        y_attn_n = rmsnorm(y_attn, g_grp3, l, 0, BF16, f"norm_attn{l}")
        y_rec_n = rmsnorm(y_rec.reshape(m, RG_WIDTH), g_grp3, l, ATTN_WIDTH // RG_WIDTH, BF16, f"norm_rec{l}")
        xf, xb, xr = matmul([y_attn_n, y_rec_n, y_cv_n.reshape(m, CV_WIDTH)], w_out, l, F32, f"out_proj{l}",
                            tn=512, tm=1024, res=xf, emit_norm=True)

        mn = rmsnorm(memf, g_mem3, l, 0, BF16, f"norm_mem{l}")
        q2 = matmul([xb], xa_wq, l, BF16, f"xa_q{l}", tn=512, tm=1024, norm_in=(g_xattn_c, xr))
        k2 = matmul([mn], xa_wk, l, BF16, f"xa_k{l}", tn=512, tm=1024)
        v2 = matmul([mn], xa_wv, l, BF16, f"xa_v{l}", tn=512, tm=1024)
        o2 = cross_attention(q2, k2, v2, b, seq, n_mem, f"xattn{l}")
        xf, xb, xr = matmul([o2], xa_wo, l, F32, f"xa_o{l}", tn=512, tm=1024, res=xf, emit_norm=True)

        act = swiglu(xb, ffn_wg, ffn_wu, g_ffn_c, xr, l, f"ffn_up{l}", tn=256, tm=1024)
        if last:
            xf = matmul([act], wd_bf16, l, F32, f"ffn_down{l}", tn=512, tm=512, res=xf)
        else:
            xf, xb, xr = matmul([act], wd_bf16, l, F32, f"ffn_down{l}", tn=512, tm=512, res=xf, emit_norm=True)

    out = rmsnorm(xf, g_final3, 0, 0, F32, "norm_final")
    return out.reshape(b, seq, d)
```

```python
import functools
import math

import jax
import jax.numpy as jnp
from jax import lax
from jax.experimental import pallas as pl
from jax.experimental.pallas import tpu as pltpu

F32 = jnp.float32
BF16 = jnp.bfloat16

HEAD_DIM = 128
N_Q_HEADS = 16
N_KV_HEADS = 4
Q_PER_KV = N_Q_HEADS // N_KV_HEADS
ATTN_WIDTH = N_Q_HEADS * HEAD_DIM
KV_WIDTH = N_KV_HEADS * HEAD_DIM
RG_WIDTH = 1024
RG_BLOCK = 128
RG_CONV_W = 4
RG_C = 8.0
CV_WIDTH = 1024
CV_KERNEL = 31
QKV_WIDTH = ATTN_WIDTH + 2 * KV_WIDTH
COL_RX = QKV_WIDTH // RG_WIDTH
COL_RG = COL_RX + 1
COL_CA = COL_RG + 1
COL_CB = COL_CA + 1
MEM_HEADS = 4
GRID_W = 64
ROPE_THETA = 10000.0
ROPE_AXIS_DIM = HEAD_DIM // 2
EPS = 1e-6

V7X_VMEM_BYTES = 64 * 1024 * 1024
V7X_SUBLANES = 8
V7X_LANES = 128
VMEM_LIMIT = V7X_VMEM_BYTES - 4 * 1024 * 1024

CV_HALO = 16


def _sigmoid(x):
    return 0.5 * jnp.tanh(0.5 * x) + 0.5


def _cparams(sem):
    return pltpu.CompilerParams(dimension_semantics=sem, vmem_limit_bytes=VMEM_LIMIT)


def _rmsnorm_kernel(x_ref, g_ref, o_ref):
    x = x_ref[...].astype(F32)
    ms = jnp.mean(x * x, axis=-1, keepdims=True)
    o_ref[...] = (x * lax.rsqrt(ms + EPS) * g_ref[...]).astype(o_ref.dtype)


def rmsnorm(x, g3, layer, gcol, out_dtype, name, tm=256):
    m, w = x.shape
    tm = min(tm, m)
    return pl.pallas_call(
        _rmsnorm_kernel,
        out_shape=jax.ShapeDtypeStruct((m, w), out_dtype),
        grid=(m // tm,),
        in_specs=[
            pl.BlockSpec((tm, w), lambda i: (i, 0)),
            pl.BlockSpec((None, 1, w), lambda i: (layer, 0, gcol)),
        ],
        out_specs=pl.BlockSpec((tm, w), lambda i: (i, 0)),
        compiler_params=_cparams(("arbitrary",)),
        name=name,
    )(x, g3)


CAST_ROWS = 256


def _cast_panel(w_ref, wb_ref, g_ref=None):
    k, tn = w_ref.shape

    def body(i, c):
        rows = pl.ds(pl.multiple_of(i * CAST_ROWS, CAST_ROWS), CAST_ROWS)
        if g_ref is None:
            wb_ref[rows, :] = w_ref[rows, :].astype(BF16)
        else:
            g = g_ref[rows, :]
            for c0 in range(0, tn, V7X_LANES):
                wb_ref[rows, c0:c0 + V7X_LANES] = (w_ref[rows, c0:c0 + V7X_LANES] * g).astype(BF16)
        return c

    lax.fori_loop(0, k // CAST_ROWS, body, 0)


STAT_LANES = V7X_LANES


def _mm_kernel(*refs, k_splits, has_norm_in, has_res, emit_norm, w_is_bf16, n_total):
    n_a = len(k_splits)
    a_refs = refs[:n_a]
    w_ref = refs[n_a]
    pos = n_a + 1
    g_ref = rin_ref = res_ref = xb_ref = rout_ref = ssq_ref = None
    if has_norm_in:
        g_ref, rin_ref = refs[pos], refs[pos + 1]
        pos += 2
    if has_res:
        res_ref = refs[pos]
        pos += 1
    o_ref = refs[pos]
    pos += 1
    if emit_norm:
        xb_ref, rout_ref = refs[pos], refs[pos + 1]
        pos += 2
    j = pl.program_id(0)
    i = pl.program_id(1)
    if w_is_bf16:
        wb_ref = w_ref
    else:
        wb_ref = refs[pos]
        pos += 1

        @pl.when(i == 0)
        def _():
            _cast_panel(w_ref, wb_ref, g_ref)

    acc = None
    off = 0
    for a_ref, kk in zip(a_refs, k_splits):
        d = jnp.dot(a_ref[...], wb_ref[off:off + kk, :], preferred_element_type=F32)
        acc = d if acc is None else acc + d
        off += kk
    if has_norm_in:
        acc = acc * rin_ref[:, 0:1]
    if has_res:
        acc = acc + res_ref[...]
    o_ref[...] = acc.astype(o_ref.dtype)
    if emit_norm:
        ssq_ref = refs[pos]
        tm = acc.shape[0]
        xb_ref[...] = acc.astype(BF16)
        rows = pl.ds(pl.multiple_of(i * tm, tm), tm)
        part = jnp.broadcast_to(jnp.sum(acc * acc, axis=-1, keepdims=True), (tm, STAT_LANES))

        @pl.when(j == 0)
        def _():
            ssq_ref[rows, :] = part

        @pl.when(j > 0)
        def _():
            ssq_ref[rows, :] = ssq_ref[rows, :] + part

        rout_ref[...] = lax.rsqrt(ssq_ref[rows, :] * (1.0 / n_total) + EPS)


def matmul(a_list, w3, layer, out_dtype, name, *, tn, tm, norm_in=None, res=None, emit_norm=False,
           single_buffer_w=False):
    m = a_list[0].shape[0]
    k_splits = tuple(a.shape[1] for a in a_list)
    _, k, n = w3.shape
    assert sum(k_splits) == k
    tm = min(tm, m)
    w_is_bf16 = w3.dtype == BF16
    assert not (w_is_bf16 and norm_in is not None)
    in_specs = [pl.BlockSpec((tm, kk), lambda j, i: (i, 0)) for kk in k_splits]
    w_mode = dict(pipeline_mode=pl.Buffered(1)) if single_buffer_w else {}
    in_specs.append(pl.BlockSpec((None, k, tn), lambda j, i: (layer, 0, j), **w_mode))
    args = list(a_list) + [w3]
    if norm_in is not None:
        last_slab = norm_in[1].shape[0] - 1
        in_specs.append(pl.BlockSpec((None, k, V7X_LANES), lambda j, i: (layer, 0, 0)))
        in_specs.append(pl.BlockSpec((None, tm, STAT_LANES), lambda j, i: (last_slab, i, 0)))
        args += list(norm_in)
    if res is not None:
        in_specs.append(pl.BlockSpec((tm, tn), lambda j, i: (i, j)))
        args.append(res)
    out_shape = [jax.ShapeDtypeStruct((m, n), out_dtype)]
    out_specs = [pl.BlockSpec((tm, tn), lambda j, i: (i, j))]
    scratch = [] if w_is_bf16 else [pltpu.VMEM((k, tn), BF16)]
    if emit_norm:
        out_shape += [jax.ShapeDtypeStruct((m, n), BF16), jax.ShapeDtypeStruct((n // tn, m, STAT_LANES), F32)]
        out_specs += [pl.BlockSpec((tm, tn), lambda j, i: (i, j)),
                      pl.BlockSpec((None, tm, STAT_LANES), lambda j, i: (j, i, 0))]
        scratch.append(pltpu.VMEM((m, STAT_LANES), F32))
    kern = functools.partial(_mm_kernel, k_splits=k_splits, has_norm_in=norm_in is not None,
                             has_res=res is not None, emit_norm=emit_norm, w_is_bf16=w_is_bf16, n_total=n)
    out = pl.pallas_call(
        kern,
        out_shape=tuple(out_shape),
        grid=(n // tn, m // tm),
        in_specs=in_specs,
        out_specs=tuple(out_specs),
        scratch_shapes=scratch,
        compiler_params=_cparams(("arbitrary", "arbitrary")),
        name=name,
    )(*args)
    return out if emit_norm else out[0]


def _row_stats_kernel(x_ref, xb_ref, r_ref):
    x = x_ref[...]
    xb_ref[...] = x.astype(BF16)
    ms = jnp.mean(x * x, axis=-1, keepdims=True)
    r_ref[...] = jnp.broadcast_to(lax.rsqrt(ms + EPS), r_ref.shape)


def row_stats(x, name, tm=256):
    m, w = x.shape
    tm = min(tm, m)
    return pl.pallas_call(
        _row_stats_kernel,
        out_shape=(jax.ShapeDtypeStruct((m, w), BF16), jax.ShapeDtypeStruct((1, m, STAT_LANES), F32)),
        grid=(m // tm,),
        in_specs=[pl.BlockSpec((tm, w), lambda i: (i, 0))],
        out_specs=(pl.BlockSpec((tm, w), lambda i: (i, 0)),
                   pl.BlockSpec((None, tm, STAT_LANES), lambda i: (0, i, 0))),
        compiler_params=_cparams(("arbitrary",)),
        name=name,
    )(x)


def _swiglu_kernel(a_ref, wg_ref, wu_ref, g_ref, rin_ref, o_ref, wgb_ref, wub_ref):
    @pl.when(pl.program_id(1) == 0)
    def _():
        _cast_panel(wg_ref, wgb_ref, g_ref)
        _cast_panel(wu_ref, wub_ref, g_ref)

    a = a_ref[...]
    r = rin_ref[:, 0:1]
    g = jnp.dot(a, wgb_ref[...], preferred_element_type=F32) * r
    u = jnp.dot(a, wub_ref[...], preferred_element_type=F32) * r
    o_ref[...] = (g * _sigmoid(g) * u).astype(o_ref.dtype)


def swiglu(a, wg3, wu3, g3, rin, layer, name, *, tn, tm):
    m, k = a.shape
    n = wg3.shape[2]
    tm = min(tm, m)
    assert n % tn == 0
    wspec = pl.BlockSpec((None, k, tn), lambda j, i: (layer, 0, j))
    last_slab = rin.shape[0] - 1
    return pl.pallas_call(
        _swiglu_kernel,
        out_shape=jax.ShapeDtypeStruct((m, n), BF16),
        grid=(n // tn, m // tm),
        in_specs=[pl.BlockSpec((tm, k), lambda j, i: (i, 0)), wspec, wspec,
                  pl.BlockSpec((None, k, V7X_LANES), lambda j, i: (layer, 0, 0)),
                  pl.BlockSpec((None, tm, STAT_LANES), lambda j, i: (last_slab, i, 0))],
        out_specs=pl.BlockSpec((tm, tn), lambda j, i: (i, j)),
        scratch_shapes=[pltpu.VMEM((k, tn), BF16), pltpu.VMEM((k, tn), BF16)],
        compiler_params=_cparams(("arbitrary", "arbitrary")),
        name=name,
    )(a, wg3, wu3, g3, rin)


def _cast_kernel(x_ref, o_ref):
    o_ref[...] = x_ref[...].astype(o_ref.dtype)


def cast_bf16(w3, name, rows=256):
    l, k, n = w3.shape
    return pl.pallas_call(
        _cast_kernel,
        out_shape=jax.ShapeDtypeStruct((l, k, n), BF16),
        grid=(l, k // rows),
        in_specs=[pl.BlockSpec((None, rows, n), lambda a, i: (a, i, 0))],
        out_specs=pl.BlockSpec((None, rows, n), lambda a, i: (a, i, 0)),
        compiler_params=_cparams(("arbitrary", "arbitrary")),
        name=name,
    )(w3)


def rope_tables(seq):
    rows = seq // GRID_W
    row = jnp.repeat(jnp.arange(rows), GRID_W).astype(F32)
    col = jnp.tile(jnp.arange(GRID_W), rows).astype(F32)
    inv = ROPE_THETA ** (-jnp.arange(0, ROPE_AXIS_DIM, 2, dtype=F32) / ROPE_AXIS_DIM)
    ang_r = row[:, None] * inv
    ang_c = col[:, None] * inv
    cos_t = jnp.concatenate([jnp.cos(ang_r), jnp.cos(ang_r), jnp.cos(ang_c), jnp.cos(ang_c)], axis=1)
    sin_t = jnp.concatenate([-jnp.sin(ang_r), jnp.sin(ang_r), -jnp.sin(ang_c), jnp.sin(ang_c)], axis=1)
    return cos_t, sin_t


def _norm_rope(x, g, cos, sin):
    half = ROPE_AXIS_DIM // 2
    lane = lax.broadcasted_iota(jnp.int32, (1, HEAD_DIM), 1)
    first_half = (lane % ROPE_AXIS_DIM) < half
    ms = jnp.mean(x * x, axis=-1, keepdims=True)
    xn = x * lax.rsqrt(ms + EPS) * g
    swapped = jnp.where(first_half, pltpu.roll(xn, HEAD_DIM - half, axis=1), pltpu.roll(xn, half, axis=1))
    return xn * cos + swapped * sin


def _attn_kernel(zq_ref, zk_ref, zv_ref, gq_ref, gk_ref, cos_ref, sin_ref, o_ref, k_s, v_s, *, tq, seq):
    i = pl.program_id(2)

    @pl.when(i == 0)
    def _():
        k_s[...] = _norm_rope(zk_ref[...], gk_ref[...], cos_ref[...], sin_ref[...]).astype(BF16)
        v_s[:, 0:HEAD_DIM] = zv_ref[...].astype(BF16)
        v_s[:, HEAD_DIM:2 * HEAD_DIM] = jnp.ones((seq, HEAD_DIM), BF16)

    r0 = pl.multiple_of(i * tq, tq)
    cos = cos_ref[pl.ds(r0, tq), :]
    sin = sin_ref[pl.ds(r0, tq), :]
    gq = gq_ref[...]
    k = k_s[...]
    v = v_s[...]
    c = (HEAD_DIM ** -0.5) * math.log2(math.e)
    for h in range(Q_PER_KV):
        c0 = h * HEAD_DIM
        q = _norm_rope(zq_ref[:, c0:c0 + HEAD_DIM], gq, cos, sin).astype(BF16)
        s = lax.dot_general(q, k, (((1,), (1,)), ((), ())), preferred_element_type=F32)
        mx = jnp.max(s, axis=-1, keepdims=True)
        p = jnp.exp2((s - mx) * c).astype(BF16)
        o = jnp.dot(p, v, preferred_element_type=F32)
        o_ref[:, c0:c0 + HEAD_DIM] = (o[:, 0:HEAD_DIM] / o[:, HEAD_DIM:2 * HEAD_DIM]).astype(o_ref.dtype)


def attention(z, gq3, gk3, cos_t, sin_t, layer, batch, seq, name, tq=256):
    m = z.shape[0]
    tq = min(tq, seq)
    nq = seq // tq
    gw = Q_PER_KV * HEAD_DIM
    kcol = ATTN_WIDTH // HEAD_DIM
    vcol = (ATTN_WIDTH + KV_WIDTH) // HEAD_DIM
    gspec = pl.BlockSpec((None, 1, HEAD_DIM), lambda b, g, i: (layer, 0, 0))
    tspec = pl.BlockSpec((seq, HEAD_DIM), lambda b, g, i: (0, 0))
    kern = functools.partial(_attn_kernel, tq=tq, seq=seq)
    return pl.pallas_call(
        kern,
        out_shape=jax.ShapeDtypeStruct((m, ATTN_WIDTH), F32),
        grid=(batch, N_KV_HEADS, nq),
        in_specs=[
            pl.BlockSpec((tq, gw), lambda b, g, i: (b * nq + i, g)),
            pl.BlockSpec((seq, HEAD_DIM), lambda b, g, i: (b, kcol + g)),
            pl.BlockSpec((seq, HEAD_DIM), lambda b, g, i: (b, vcol + g)),
            gspec, gspec, tspec, tspec,
        ],
        out_specs=pl.BlockSpec((tq, gw), lambda b, g, i: (b * nq + i, g)),
        scratch_shapes=[pltpu.VMEM((seq, HEAD_DIM), BF16), pltpu.VMEM((seq, 2 * HEAD_DIM), BF16)],
        compiler_params=_cparams(("arbitrary", "arbitrary", "arbitrary")),
        name=name,
    )(z, z, z, gq3, gk3, cos_t, sin_t)


def _xq_attn_kernel(x_ref, w_ref, g_ref, rin_ref, k_ref, v_ref, o_ref, wb_ref):
    @pl.when(pl.program_id(1) == 0)
    def _():
        _cast_panel(w_ref, wb_ref, g_ref)

    head_dim = wb_ref.shape[1]
    q = jnp.dot(x_ref[...], wb_ref[...], preferred_element_type=F32) * rin_ref[:, 0:1]
    k = k_ref[...]
    v = v_ref[...]
    s = lax.dot_general(q.astype(BF16), k, (((1,), (1,)), ((), ())), preferred_element_type=F32)
    s = s * (head_dim ** -0.5)
    mx = jnp.max(s, axis=-1, keepdims=True)
    p = jnp.exp(s - mx)
    den = jnp.sum(p, axis=-1, keepdims=True)
    o = jnp.dot(p.astype(BF16), v, preferred_element_type=F32)
    o_ref[...] = (o / den).astype(o_ref.dtype)


def q_proj_cross_attention(xb, wq3, g3, rin, k, v, layer, seq, n_mem, name, tm=512):
    m, d = xb.shape
    head_dim = d // MEM_HEADS
    tm = min(tm, seq)
    tiles_per_batch = seq // tm
    last_slab = rin.shape[0] - 1
    kvspec = pl.BlockSpec((n_mem, head_dim), lambda j, i: (i // tiles_per_batch, j))
    return pl.pallas_call(
        _xq_attn_kernel,
        out_shape=jax.ShapeDtypeStruct((m, d), BF16),
        grid=(MEM_HEADS, m // tm),
        in_specs=[
            pl.BlockSpec((tm, d), lambda j, i: (i, 0)),
            pl.BlockSpec((None, d, head_dim), lambda j, i: (layer, 0, j)),
            pl.BlockSpec((None, d, V7X_LANES), lambda j, i: (layer, 0, 0)),
            pl.BlockSpec((None, tm, STAT_LANES), lambda j, i: (last_slab, i, 0)),
            kvspec, kvspec,
        ],
        out_specs=pl.BlockSpec((tm, head_dim), lambda j, i: (i, j)),
        scratch_shapes=[pltpu.VMEM((d, head_dim), BF16)],
        compiler_params=_cparams(("arbitrary", "arbitrary")),
        name=name,
    )(xb, wq3, g3, rin, k, v)


RG_PAD = V7X_SUBLANES


def _rglru_kernel(x_ref, gate_ref, cw_ref, cb_ref, w_ref, ba_ref, bx_ref, lam_ref, y_ref,
                  pad_ref, af_ref, uf_ref, ab_ref, ub_ref, *, seq, heads):
    cw = heads * RG_BLOCK
    zeros = jnp.zeros((RG_PAD, cw), F32)
    pad_ref[0:RG_PAD, :] = zeros
    pad_ref[RG_PAD + seq:RG_PAD + seq + RG_PAD, :] = zeros
    pad_ref[RG_PAD:RG_PAD + seq, :] = x_ref[...]
    left = RG_CONV_W // 2
    xc = cb_ref[...] + cw_ref[0:1, :] * pad_ref[pl.ds(RG_PAD - left, seq), :]
    for j in range(1, RG_CONV_W):
        xc = xc + cw_ref[j:j + 1, :] * pad_ref[pl.ds(RG_PAD - left + j, seq), :]
    xcb = xc.astype(BF16)

    lam = lam_ref[...]
    nl = -lam
    softplus = jnp.maximum(nl, 0.0) + jnp.log1p(jnp.exp(-jnp.abs(nl)))
    ba = ba_ref[...]
    bx = bx_ref[...]
    for h in range(heads):
        c0 = h * RG_BLOCK
        sl = slice(c0, c0 + RG_BLOCK)
        gates = jnp.dot(xcb[:, sl], w_ref[h].astype(BF16), preferred_element_type=F32)
        xch = xc[:, sl]
        for d, (a_ref, u_ref) in enumerate(((af_ref, uf_ref), (ab_ref, ub_ref))):
            g0 = 2 * d * RG_BLOCK
            r = _sigmoid(gates[:, g0:g0 + RG_BLOCK] + ba[d:d + 1, sl])
            i = _sigmoid(gates[:, g0 + RG_BLOCK:g0 + 2 * RG_BLOCK] + bx[d:d + 1, sl])
            log_a = (-RG_C) * r * softplus[d:d + 1, sl]
            t = jnp.tanh(log_a)
            one_minus_a2 = (-2.0 * t) / (1.0 - t)
            a_ref[:, sl] = jnp.exp(log_a)
            u_ref[:, sl] = jnp.sqrt(one_minus_a2) * (i * xch)

    n_tiles = seq // V7X_SUBLANES
    row = lax.broadcasted_iota(jnp.int32, (V7X_SUBLANES, cw), 0)

    def tile_scan(a, u, reverse):
        for d in (1, 2, 4):
            if reverse:
                shift, keep = V7X_SUBLANES - d, row < V7X_SUBLANES - d
            else:
                shift, keep = d, row >= d
            a_s = pltpu.roll(a, shift, axis=0)
            u_s = pltpu.roll(u, shift, axis=0)
            u = jnp.where(keep, u + a * u_s, u)
            a = jnp.where(keep, a * a_s, a)
        return a, u

    def body(t, carry):
        hf, hb = carry
        rf = pl.multiple_of(t * V7X_SUBLANES, V7X_SUBLANES)
        a, u = tile_scan(af_ref[pl.ds(rf, V7X_SUBLANES), :], uf_ref[pl.ds(rf, V7X_SUBLANES), :], False)
        hh = a * hf + u
        uf_ref[pl.ds(rf, V7X_SUBLANES), :] = hh
        hf = jnp.broadcast_to(hh[V7X_SUBLANES - 1:V7X_SUBLANES, :], hh.shape)
        rb = pl.multiple_of((n_tiles - 1 - t) * V7X_SUBLANES, V7X_SUBLANES)
        a, u = tile_scan(ab_ref[pl.ds(rb, V7X_SUBLANES), :], ub_ref[pl.ds(rb, V7X_SUBLANES), :], True)
        hh = a * hb + u
        ub_ref[pl.ds(rb, V7X_SUBLANES), :] = hh
        hb = jnp.broadcast_to(hh[0:1, :], hh.shape)
        return hf, hb

    h0 = jnp.zeros((V7X_SUBLANES, cw), F32)
    lax.fori_loop(0, n_tiles, body, (h0, h0))

    g = gate_ref[...]
    gelu = 0.5 * g * (1.0 + jnp.tanh(math.sqrt(2.0 / math.pi) * (g + 0.044715 * (g * g * g))))
    y_ref[...] = ((uf_ref[...] + ub_ref[...]) * gelu).astype(y_ref.dtype)


def rglru(z3, cw3, cb3, w4, ba3, bx3, lam3, layer, name, heads=2):
    b, seq, _ = z3.shape
    cw = heads * RG_BLOCK
    ncb = RG_WIDTH // cw
    n_rg_heads = RG_WIDTH // RG_BLOCK
    kern = functools.partial(_rglru_kernel, seq=seq, heads=heads)
    vec = lambda rows: pl.BlockSpec((None, rows, cw), lambda bi, j: (layer, 0, j))
    return pl.pallas_call(
        kern,
        out_shape=jax.ShapeDtypeStruct((b, seq, RG_WIDTH), F32),
        grid=(b, ncb),
        in_specs=[
            pl.BlockSpec((None, seq, cw), lambda bi, j: (bi, 0, COL_RX * ncb + j)),
            pl.BlockSpec((None, seq, cw), lambda bi, j: (bi, 0, COL_RG * ncb + j)),
            vec(RG_CONV_W), vec(1),
            pl.BlockSpec((heads, RG_BLOCK, 4 * RG_BLOCK), lambda bi, j: ((layer * n_rg_heads) // heads + j, 0, 0)),
            vec(2), vec(2), vec(2),
        ],
        out_specs=pl.BlockSpec((None, seq, cw), lambda bi, j: (bi, 0, j)),
        scratch_shapes=[pltpu.VMEM((seq + 2 * RG_PAD, cw), F32)] + [pltpu.VMEM((seq, cw), F32)] * 4,
        compiler_params=_cparams(("arbitrary", "arbitrary")),
        name=name,
    )(z3, z3, cw3, cb3, w4, ba3, bx3, lam3)


CV_CHUNK = 32


def _conformer_kernel(a_ref, g_ref, ap_ref, gp_ref, an_ref, gn_ref, dw_ref, db_ref, lg_ref, lb_ref,
                      pw_ref, pb_ref, gg_ref, o_ref, buf_ref, sh_ref, conv_ref, pwb_ref, dwb_ref, *, ts):
    i = pl.program_id(1)
    n_i = pl.num_programs(1)

    @pl.when(jnp.logical_and(pl.program_id(0) == 0, i == 0))
    def _():
        pwb_ref[...] = pw_ref[...].astype(BF16)
        for j in range(CV_KERNEL):
            dwb_ref[j] = jnp.broadcast_to(dw_ref[j:j + 1, :], (V7X_SUBLANES, CV_WIDTH))

    def glu(a, g):
        return a * _sigmoid(g)

    prev = glu(ap_ref[...], gp_ref[...])
    nxt = glu(an_ref[...], gn_ref[...])
    buf_ref[0:CV_HALO, :] = jnp.where(i > 0, prev, 0.0)
    buf_ref[CV_HALO:CV_HALO + ts, :] = glu(a_ref[...], g_ref[...])
    buf_ref[CV_HALO + ts:CV_HALO + ts + CV_HALO, :] = jnp.where(i < n_i - 1, nxt, 0.0)

    span = ts + 2 * CV_HALO - V7X_SUBLANES
    for r in range(V7X_SUBLANES):
        sh_ref[r] = buf_ref[pl.ds(r, span), :]

    base_off = CV_HALO - CV_KERNEL // 2
    n_sub = CV_CHUNK // V7X_SUBLANES
    bias = jnp.broadcast_to(db_ref[...], (V7X_SUBLANES, CV_WIDTH))

    def chunk(c, carry):
        r0 = pl.multiple_of(c * CV_CHUNK, CV_CHUNK)
        accs = [bias] * n_sub
        for j in range(CV_KERNEL):
            off = base_off + j
            q, r = off // V7X_SUBLANES, off % V7X_SUBLANES
            w = dwb_ref[j]
            for s in range(n_sub):
                rows = pl.multiple_of(r0 + (q + s) * V7X_SUBLANES, V7X_SUBLANES)
                accs[s] = accs[s] + w * sh_ref[r, pl.ds(rows, V7X_SUBLANES), :]
        for s in range(n_sub):
            rows = pl.multiple_of(r0 + s * V7X_SUBLANES, V7X_SUBLANES)
            conv_ref[pl.ds(rows, V7X_SUBLANES), :] = accs[s]
        return carry

    lax.fori_loop(0, ts // CV_CHUNK, chunk, 0)

    y = conv_ref[...]
    mu = jnp.mean(y, axis=-1, keepdims=True)
    yc = y - mu
    var = jnp.mean(yc * yc, axis=-1, keepdims=True)
    yn = yc * lax.rsqrt(var + EPS) * lg_ref[...] + lb_ref[...]
    act = yn * _sigmoid(yn)
    out = jnp.dot(act.astype(BF16), pwb_ref[...], preferred_element_type=F32) + pb_ref[...]
    ms = jnp.mean(out * out, axis=-1, keepdims=True)
    o_ref[...] = (out * lax.rsqrt(ms + EPS) * gg_ref[...]).astype(o_ref.dtype)


def conformer(z3, dw3, db3, lg3, lb3, pw3, pb3, gg3, layer, name, ts=256):
    b, seq, _ = z3.shape
    ts = min(ts, seq)
    nt = seq // ts
    hb = ts // CV_HALO
    n_hblk = seq // CV_HALO
    main = lambda col: pl.BlockSpec((None, ts, CV_WIDTH), lambda bi, i: (bi, i, col))
    prev = lambda col: pl.BlockSpec((None, CV_HALO, CV_WIDTH),
                                    lambda bi, i: (bi, jnp.maximum(i * hb - 1, 0), col))
    nxt = lambda col: pl.BlockSpec((None, CV_HALO, CV_WIDTH),
                                   lambda bi, i: (bi, jnp.minimum((i + 1) * hb, n_hblk - 1), col))
    vec = lambda rows: pl.BlockSpec((None, rows, CV_WIDTH), lambda bi, i: (layer, 0, 0))
    kern = functools.partial(_conformer_kernel, ts=ts)
    return pl.pallas_call(
        kern,
        out_shape=jax.ShapeDtypeStruct((b, seq, CV_WIDTH), BF16),
        grid=(b, nt),
        in_specs=[
            main(COL_CA), main(COL_CB), prev(COL_CA), prev(COL_CB), nxt(COL_CA), nxt(COL_CB),
            vec(CV_KERNEL), vec(1), vec(1), vec(1),
            pl.BlockSpec((None, CV_WIDTH, CV_WIDTH), lambda bi, i: (layer, 0, 0)),
            vec(1),
            pl.BlockSpec((None, 1, CV_WIDTH), lambda bi, i: (layer, 0, (ATTN_WIDTH + RG_WIDTH) // CV_WIDTH)),
        ],
        out_specs=pl.BlockSpec((None, ts, CV_WIDTH), lambda bi, i: (bi, i, 0)),
        scratch_shapes=[
            pltpu.VMEM((ts + 2 * CV_HALO, CV_WIDTH), F32),
            pltpu.VMEM((V7X_SUBLANES, ts + 2 * CV_HALO - V7X_SUBLANES, CV_WIDTH), F32),
            pltpu.VMEM((ts, CV_WIDTH), F32),
            pltpu.VMEM((CV_WIDTH, CV_WIDTH), BF16),
            pltpu.VMEM((CV_KERNEL, V7X_SUBLANES, CV_WIDTH), F32),
        ],
        compiler_params=_cparams(("arbitrary", "arbitrary")),
        name=name,
    )(z3, z3, z3, z3, z3, z3, dw3, db3, lg3, lb3, pw3, pb3, gg3)


def kernel(x, mem, g_mix, w_in, g_q, g_k, rg_conv_w, rg_conv_b, rg_w_a, rg_b_a, rg_w_x, rg_b_x, rg_lam, cv_dw_w, cv_dw_b, cv_ln_g, cv_ln_b, cv_pw_w, cv_pw_b, g_grp, w_out, g_xattn, g_mem, xa_wq, xa_wk, xa_wv, xa_wo, g_ffn, ffn_wg, ffn_wu, ffn_wd, g_final):
    b, seq, d = x.shape
    n_mem = mem.shape[1]
    depth = w_in.shape[0]
    m = b * seq
    row3 = lambda p: p.reshape(p.shape[0], 1, p.shape[-1])

    cos_t, sin_t = rope_tables(seq)
    g_q3, g_k3, g_grp3, g_mem3 = row3(g_q), row3(g_k), row3(g_grp), row3(g_mem)
    g_final3 = g_final.reshape(1, 1, d)
    rg_cb3, cv_db3, cv_lg3, cv_lb3, cv_pb3 = (row3(rg_conv_b), row3(cv_dw_b), row3(cv_ln_g),
                                              row3(cv_ln_b), row3(cv_pw_b))
    rg_w4 = jnp.concatenate([rg_w_a[:, 0], rg_w_x[:, 0], rg_w_a[:, 1], rg_w_x[:, 1]], axis=-1)
    rg_w4 = rg_w4.reshape(-1, RG_BLOCK, 4 * RG_BLOCK)

    col3 = lambda p: jnp.broadcast_to(p[:, :, None], (p.shape[0], p.shape[1], V7X_LANES))
    g_mix_c, g_xattn_c, g_ffn_c = col3(g_mix), col3(g_xattn), col3(g_ffn)
    wd_bf16 = cast_bf16(ffn_wd, "cast_wd")

    xf = x.reshape(m, d)
    memf = mem.reshape(b * n_mem, d)
    xb, xr = row_stats(xf, "x_stats")

    for l in range(depth):
        last = l == depth - 1
        z = matmul([xb], w_in, l, F32, f"in_proj{l}", tn=1024, tm=512, norm_in=(g_mix_c, xr))
        z3 = z.reshape(b, seq, z.shape[1])
        y_attn = attention(z, g_q3, g_k3, cos_t, sin_t, l, b, seq, f"attn{l}")
        y_rec = rglru(z3, rg_conv_w, rg_cb3, rg_w4, rg_b_a, rg_b_x, rg_lam, l, f"rglru{l}")
        y_cv_n = conformer(z3, cv_dw_w, cv_db3, cv_lg3, cv_lb3, cv_pw_w, cv_pb3, g_grp3, l, f"conformer{l}")
        y_attn_n = rmsnorm(y_attn, g_grp3, l, 0, BF16, f"norm_attn{l}")
        y_rec_n = rmsnorm(y_rec.reshape(m, RG_WIDTH), g_grp3, l, ATTN_WIDTH // RG_WIDTH, BF16, f"norm_rec{l}")
        xf, xb, xr = matmul([y_attn_n, y_rec_n, y_cv_n.reshape(m, CV_WIDTH)], w_out, l, F32, f"out_proj{l}",
                            tn=1024, tm=512, res=xf, emit_norm=True, single_buffer_w=True)

        mn = rmsnorm(memf, g_mem3, l, 0, BF16, f"norm_mem{l}")
        k2 = matmul([mn], xa_wk, l, BF16, f"xa_k{l}", tn=512, tm=1024)
        v2 = matmul([mn], xa_wv, l, BF16, f"xa_v{l}", tn=512, tm=1024)
        o2 = q_proj_cross_attention(xb, xa_wq, g_xattn_c, xr, k2, v2, l, seq, n_mem, f"xa_q_attn{l}")
        xf, xb, xr = matmul([o2], xa_wo, l, F32, f"xa_o{l}", tn=1024, tm=512, res=xf, emit_norm=True,
                            single_buffer_w=True)

        act = swiglu(xb, ffn_wg, ffn_wu, g_ffn_c, xr, l, f"ffn_up{l}", tn=256, tm=1024)
        if last:
            xf = matmul([act], wd_bf16, l, F32, f"ffn_down{l}", tn=512, tm=512, res=xf)
        else:
            xf, xb, xr = matmul([act], wd_bf16, l, F32, f"ffn_down{l}", tn=512, tm=512, res=xf, emit_norm=True)

    out = rmsnorm(xf, g_final3, 0, 0, F32, "norm_final")
    return out.reshape(b, seq, d)
```

```python
import functools
import math

import jax
import jax.numpy as jnp
from jax import lax
from jax.experimental import pallas as pl
from jax.experimental.pallas import tpu as pltpu

F32 = jnp.float32
BF16 = jnp.bfloat16

HEAD_DIM = 128
N_Q_HEADS = 16
N_KV_HEADS = 4
Q_PER_KV = N_Q_HEADS // N_KV_HEADS
ATTN_WIDTH = N_Q_HEADS * HEAD_DIM
KV_WIDTH = N_KV_HEADS * HEAD_DIM
RG_WIDTH = 1024
RG_BLOCK = 128
RG_CONV_W = 4
RG_C = 8.0
CV_WIDTH = 1024
CV_KERNEL = 31
QKV_WIDTH = ATTN_WIDTH + 2 * KV_WIDTH
COL_RX = QKV_WIDTH // RG_WIDTH
COL_RG = COL_RX + 1
COL_CA = COL_RG + 1
COL_CB = COL_CA + 1
MEM_HEADS = 4
GRID_W = 64
ROPE_THETA = 10000.0
ROPE_AXIS_DIM = HEAD_DIM // 2
EPS = 1e-6

V7X_VMEM_BYTES = 64 * 1024 * 1024
V7X_SUBLANES = 8
V7X_LANES = 128
VMEM_LIMIT = V7X_VMEM_BYTES - 4 * 1024 * 1024

CV_HALO = 16


def _sigmoid(x):
    return 0.5 * jnp.tanh(0.5 * x) + 0.5


def _cparams(sem):
    return pltpu.CompilerParams(dimension_semantics=sem, vmem_limit_bytes=VMEM_LIMIT)


def _rmsnorm_kernel(x_ref, g_ref, o_ref):
    x = x_ref[...].astype(F32)
    ms = jnp.mean(x * x, axis=-1, keepdims=True)
    o_ref[...] = (x * lax.rsqrt(ms + EPS) * g_ref[...]).astype(o_ref.dtype)


def rmsnorm(x, g3, layer, gcol, out_dtype, name, tm=256):
    m, w = x.shape
    tm = min(tm, m)
    return pl.pallas_call(
        _rmsnorm_kernel,
        out_shape=jax.ShapeDtypeStruct((m, w), out_dtype),
        grid=(m // tm,),
        in_specs=[
            pl.BlockSpec((tm, w), lambda i: (i, 0)),
            pl.BlockSpec((None, 1, w), lambda i: (layer, 0, gcol)),
        ],
        out_specs=pl.BlockSpec((tm, w), lambda i: (i, 0)),
        compiler_params=_cparams(("arbitrary",)),
        name=name,
    )(x, g3)


CAST_ROWS = 256
W_PREFETCH_CHUNKS = 4


def _cast_panel(w_refs, wb_ref, g_ref=None):
    kc, tn = w_refs[0].shape

    for ci, w_ref in enumerate(w_refs):
        def body(i, c, w_ref=w_ref, base=ci * kc):
            src = pl.ds(pl.multiple_of(i * CAST_ROWS, CAST_ROWS), CAST_ROWS)
            dst = pl.ds(pl.multiple_of(base + i * CAST_ROWS, CAST_ROWS), CAST_ROWS)
            if g_ref is None:
                wb_ref[dst, :] = w_ref[src, :].astype(BF16)
            else:
                g = g_ref[dst, :]
                for c0 in range(0, tn, V7X_LANES):
                    wb_ref[dst, c0:c0 + V7X_LANES] = (w_ref[src, c0:c0 + V7X_LANES] * g).astype(BF16)
            return c

        lax.fori_loop(0, kc // CAST_ROWS, body, 0)


def _n_w_chunks(requested, n_row_tiles):
    return max(1, min(requested, n_row_tiles - 1))


def _w_chunk_specs(k, tn, layer, n_panels, n_chunks):
    kc = k // n_chunks
    assert kc * n_chunks == k and kc % CAST_ROWS == 0

    def spec(c):
        def index(j, i):
            return layer, c, jnp.where(i <= c, j, jnp.minimum(j + 1, n_panels - 1))
        return pl.BlockSpec((None, kc, tn), index)

    return [spec(c) for c in range(n_chunks)]


STAT_LANES = V7X_LANES


def _mm_kernel(*refs, k_splits, n_w, has_norm_in, has_res, emit_norm, w_is_bf16, n_total):
    n_a = len(k_splits)
    a_refs = refs[:n_a]
    w_refs = refs[n_a:n_a + n_w]
    pos = n_a + n_w
    g_ref = rin_ref = res_ref = xb_ref = rout_ref = ssq_ref = None
    if has_norm_in:
        g_ref, rin_ref = refs[pos], refs[pos + 1]
        pos += 2
    if has_res:
        res_ref = refs[pos]
        pos += 1
    o_ref = refs[pos]
    pos += 1
    if emit_norm:
        xb_ref, rout_ref = refs[pos], refs[pos + 1]
        pos += 2
    j = pl.program_id(0)
    i = pl.program_id(1)
    if w_is_bf16:
        wb_ref = w_refs[0]
    else:
        wb_ref = refs[pos]
        pos += 1

        @pl.when(i == 0)
        def _():
            _cast_panel(w_refs, wb_ref, g_ref)

    acc = None
    off = 0
    for a_ref, kk in zip(a_refs, k_splits):
        d = jnp.dot(a_ref[...], wb_ref[off:off + kk, :], preferred_element_type=F32)
        acc = d if acc is None else acc + d
        off += kk
    if has_norm_in:
        acc = acc * rin_ref[:, 0:1]
    if has_res:
        acc = acc + res_ref[...]
    o_ref[...] = acc.astype(o_ref.dtype)
    if emit_norm:
        ssq_ref = refs[pos]
        tm = acc.shape[0]
        xb_ref[...] = acc.astype(BF16)
        rows = pl.ds(pl.multiple_of(i * tm, tm), tm)
        part = jnp.broadcast_to(jnp.sum(acc * acc, axis=-1, keepdims=True), (tm, STAT_LANES))

        @pl.when(j == 0)
        def _():
            ssq_ref[rows, :] = part

        @pl.when(j > 0)
        def _():
            ssq_ref[rows, :] = ssq_ref[rows, :] + part

        rout_ref[...] = lax.rsqrt(ssq_ref[rows, :] * (1.0 / n_total) + EPS)


def matmul(a_list, w3, layer, out_dtype, name, *, tn, tm, norm_in=None, res=None, emit_norm=False,
           single_buffer_w=False, w_chunks=1):
    m = a_list[0].shape[0]
    k_splits = tuple(a.shape[1] for a in a_list)
    _, k, n = w3.shape
    assert sum(k_splits) == k
    tm = min(tm, m)
    w_is_bf16 = w3.dtype == BF16
    assert not (w_is_bf16 and norm_in is not None)
    in_specs = [pl.BlockSpec((tm, kk), lambda j, i: (i, 0)) for kk in k_splits]
    assert w_chunks == 1 or not (w_is_bf16 or single_buffer_w)
    w_chunks = _n_w_chunks(w_chunks, m // tm)
    if w_chunks > 1:
        in_specs += _w_chunk_specs(k, tn, layer, n // tn, w_chunks)
    else:
        w_mode = dict(pipeline_mode=pl.Buffered(1)) if single_buffer_w else {}
        in_specs.append(pl.BlockSpec((None, k, tn), lambda j, i: (layer, 0, j), **w_mode))
    args = list(a_list) + [w3] * w_chunks
    if norm_in is not None:
        last_slab = norm_in[1].shape[0] - 1
        in_specs.append(pl.BlockSpec((None, k, V7X_LANES), lambda j, i: (layer, 0, 0)))
        in_specs.append(pl.BlockSpec((None, tm, STAT_LANES), lambda j, i: (last_slab, i, 0)))
        args += list(norm_in)
    if res is not None:
        in_specs.append(pl.BlockSpec((tm, tn), lambda j, i: (i, j)))
        args.append(res)
    out_shape = [jax.ShapeDtypeStruct((m, n), out_dtype)]
    out_specs = [pl.BlockSpec((tm, tn), lambda j, i: (i, j))]
    scratch = [] if w_is_bf16 else [pltpu.VMEM((k, tn), BF16)]
    if emit_norm:
        out_shape += [jax.ShapeDtypeStruct((m, n), BF16), jax.ShapeDtypeStruct((n // tn, m, STAT_LANES), F32)]
        out_specs += [pl.BlockSpec((tm, tn), lambda j, i: (i, j)),
                      pl.BlockSpec((None, tm, STAT_LANES), lambda j, i: (j, i, 0))]
        scratch.append(pltpu.VMEM((m, STAT_LANES), F32))
    kern = functools.partial(_mm_kernel, k_splits=k_splits, n_w=w_chunks, has_norm_in=norm_in is not None,
                             has_res=res is not None, emit_norm=emit_norm, w_is_bf16=w_is_bf16, n_total=n)
    out = pl.pallas_call(
        kern,
        out_shape=tuple(out_shape),
        grid=(n // tn, m // tm),
        in_specs=in_specs,
        out_specs=tuple(out_specs),
        scratch_shapes=scratch,
        compiler_params=_cparams(("arbitrary", "arbitrary")),
        name=name,
    )(*args)
    return out if emit_norm else out[0]


def _row_stats_kernel(x_ref, xb_ref, r_ref):
    x = x_ref[...]
    xb_ref[...] = x.astype(BF16)
    ms = jnp.mean(x * x, axis=-1, keepdims=True)
    r_ref[...] = jnp.broadcast_to(lax.rsqrt(ms + EPS), r_ref.shape)


def row_stats(x, name, tm=256):
    m, w = x.shape
    tm = min(tm, m)
    return pl.pallas_call(
        _row_stats_kernel,
        out_shape=(jax.ShapeDtypeStruct((m, w), BF16), jax.ShapeDtypeStruct((1, m, STAT_LANES), F32)),
        grid=(m // tm,),
        in_specs=[pl.BlockSpec((tm, w), lambda i: (i, 0))],
        out_specs=(pl.BlockSpec((tm, w), lambda i: (i, 0)),
                   pl.BlockSpec((None, tm, STAT_LANES), lambda i: (0, i, 0))),
        compiler_params=_cparams(("arbitrary",)),
        name=name,
    )(x)


def _swiglu_kernel(*refs, n_w):
    a_ref = refs[0]
    wg_refs = refs[1:1 + n_w]
    wu_refs = refs[1 + n_w:1 + 2 * n_w]
    g_ref, rin_ref, o_ref, wgb_ref, wub_ref = refs[1 + 2 * n_w:]

    @pl.when(pl.program_id(1) == 0)
    def _():
        _cast_panel(wg_refs, wgb_ref, g_ref)
        _cast_panel(wu_refs, wub_ref, g_ref)

    a = a_ref[...]
    r = rin_ref[:, 0:1]
    g = jnp.dot(a, wgb_ref[...], preferred_element_type=F32) * r
    u = jnp.dot(a, wub_ref[...], preferred_element_type=F32) * r
    o_ref[...] = (g * _sigmoid(g) * u).astype(o_ref.dtype)


def swiglu(a, wg3, wu3, g3, rin, layer, name, *, tn, tm, w_chunks):
    m, k = a.shape
    n = wg3.shape[2]
    tm = min(tm, m)
    assert n % tn == 0
    w_chunks = _n_w_chunks(w_chunks, m // tm)
    wspecs = _w_chunk_specs(k, tn, layer, n // tn, w_chunks)
    last_slab = rin.shape[0] - 1
    return pl.pallas_call(
        functools.partial(_swiglu_kernel, n_w=w_chunks),
        out_shape=jax.ShapeDtypeStruct((m, n), BF16),
        grid=(n // tn, m // tm),
        in_specs=[pl.BlockSpec((tm, k), lambda j, i: (i, 0))] + wspecs + wspecs + [
            pl.BlockSpec((None, k, V7X_LANES), lambda j, i: (layer, 0, 0)),
            pl.BlockSpec((None, tm, STAT_LANES), lambda j, i: (last_slab, i, 0))],
        out_specs=pl.BlockSpec((tm, tn), lambda j, i: (i, j)),
        scratch_shapes=[pltpu.VMEM((k, tn), BF16), pltpu.VMEM((k, tn), BF16)],
        compiler_params=_cparams(("arbitrary", "arbitrary")),
        name=name,
    )(a, *([wg3] * w_chunks), *([wu3] * w_chunks), g3, rin)


def _cast_kernel(x_ref, o_ref):
    o_ref[...] = x_ref[...].astype(o_ref.dtype)


def cast_bf16(w3, name, rows=256):
    l, k, n = w3.shape
    return pl.pallas_call(
        _cast_kernel,
        out_shape=jax.ShapeDtypeStruct((l, k, n), BF16),
        grid=(l, k // rows),
        in_specs=[pl.BlockSpec((None, rows, n), lambda a, i: (a, i, 0))],
        out_specs=pl.BlockSpec((None, rows, n), lambda a, i: (a, i, 0)),
        compiler_params=_cparams(("arbitrary", "arbitrary")),
        name=name,
    )(w3)


def rope_tables(seq):
    rows = seq // GRID_W
    row = jnp.repeat(jnp.arange(rows), GRID_W).astype(F32)
    col = jnp.tile(jnp.arange(GRID_W), rows).astype(F32)
    inv = ROPE_THETA ** (-jnp.arange(0, ROPE_AXIS_DIM, 2, dtype=F32) / ROPE_AXIS_DIM)
    ang_r = row[:, None] * inv
    ang_c = col[:, None] * inv
    cos_t = jnp.concatenate([jnp.cos(ang_r), jnp.cos(ang_r), jnp.cos(ang_c), jnp.cos(ang_c)], axis=1)
    sin_t = jnp.concatenate([-jnp.sin(ang_r), jnp.sin(ang_r), -jnp.sin(ang_c), jnp.sin(ang_c)], axis=1)
    return cos_t, sin_t


def _norm_rope(x, g, cos, sin):
    half = ROPE_AXIS_DIM // 2
    lane = lax.broadcasted_iota(jnp.int32, (1, HEAD_DIM), 1)
    first_half = (lane % ROPE_AXIS_DIM) < half
    ms = jnp.mean(x * x, axis=-1, keepdims=True)
    xn = x * lax.rsqrt(ms + EPS) * g
    swapped = jnp.where(first_half, pltpu.roll(xn, HEAD_DIM - half, axis=1), pltpu.roll(xn, half, axis=1))
    return xn * cos + swapped * sin


def _attn_kernel(zq_ref, zk_ref, zv_ref, gq_ref, gk_ref, cos_ref, sin_ref, o_ref, k_s, v_s, *, tq, seq):
    i = pl.program_id(2)

    @pl.when(i == 0)
    def _():
        k_s[...] = _norm_rope(zk_ref[...], gk_ref[...], cos_ref[...], sin_ref[...]).astype(BF16)
        v_s[:, 0:HEAD_DIM] = zv_ref[...].astype(BF16)
        v_s[:, HEAD_DIM:2 * HEAD_DIM] = jnp.ones((seq, HEAD_DIM), BF16)

    r0 = pl.multiple_of(i * tq, tq)
    cos = cos_ref[pl.ds(r0, tq), :]
    sin = sin_ref[pl.ds(r0, tq), :]
    gq = gq_ref[...]
    k = k_s[...]
    v = v_s[...]
    c = (HEAD_DIM ** -0.5) * math.log2(math.e)
    for h in range(Q_PER_KV):
        c0 = h * HEAD_DIM
        q = _norm_rope(zq_ref[:, c0:c0 + HEAD_DIM], gq, cos, sin).astype(BF16)
        s = lax.dot_general(q, k, (((1,), (1,)), ((), ())), preferred_element_type=F32)
        mx = jnp.max(s, axis=-1, keepdims=True)
        p = jnp.exp2((s - mx) * c).astype(BF16)
        o = jnp.dot(p, v, preferred_element_type=F32)
        o_ref[:, c0:c0 + HEAD_DIM] = (o[:, 0:HEAD_DIM] / o[:, HEAD_DIM:2 * HEAD_DIM]).astype(o_ref.dtype)


def attention(z, gq3, gk3, cos_t, sin_t, layer, batch, seq, name, tq=256):
    m = z.shape[0]
    tq = min(tq, seq)
    nq = seq // tq
    gw = Q_PER_KV * HEAD_DIM
    kcol = ATTN_WIDTH // HEAD_DIM
    vcol = (ATTN_WIDTH + KV_WIDTH) // HEAD_DIM
    gspec = pl.BlockSpec((None, 1, HEAD_DIM), lambda b, g, i: (layer, 0, 0))
    tspec = pl.BlockSpec((seq, HEAD_DIM), lambda b, g, i: (0, 0))
    kern = functools.partial(_attn_kernel, tq=tq, seq=seq)
    return pl.pallas_call(
        kern,
        out_shape=jax.ShapeDtypeStruct((m, ATTN_WIDTH), F32),
        grid=(batch, N_KV_HEADS, nq),
        in_specs=[
            pl.BlockSpec((tq, gw), lambda b, g, i: (b * nq + i, g)),
            pl.BlockSpec((seq, HEAD_DIM), lambda b, g, i: (b, kcol + g)),
            pl.BlockSpec((seq, HEAD_DIM), lambda b, g, i: (b, vcol + g)),
            gspec, gspec, tspec, tspec,
        ],
        out_specs=pl.BlockSpec((tq, gw), lambda b, g, i: (b * nq + i, g)),
        scratch_shapes=[pltpu.VMEM((seq, HEAD_DIM), BF16), pltpu.VMEM((seq, 2 * HEAD_DIM), BF16)],
        compiler_params=_cparams(("arbitrary", "arbitrary", "arbitrary")),
        name=name,
    )(z, z, z, gq3, gk3, cos_t, sin_t)


def _xq_attn_kernel(*refs, n_w):
    x_ref = refs[0]
    w_refs = refs[1:1 + n_w]
    g_ref, rin_ref, k_ref, v_ref, o_ref, wb_ref = refs[1 + n_w:]

    @pl.when(pl.program_id(1) == 0)
    def _():
        _cast_panel(w_refs, wb_ref, g_ref)

    head_dim = wb_ref.shape[1]
    q = jnp.dot(x_ref[...], wb_ref[...], preferred_element_type=F32) * rin_ref[:, 0:1]
    k = k_ref[...]
    v = v_ref[...]
    s = lax.dot_general(q.astype(BF16), k, (((1,), (1,)), ((), ())), preferred_element_type=F32)
    s = s * (head_dim ** -0.5)
    mx = jnp.max(s, axis=-1, keepdims=True)
    p = jnp.exp(s - mx)
    den = jnp.sum(p, axis=-1, keepdims=True)
    o = jnp.dot(p.astype(BF16), v, preferred_element_type=F32)
    o_ref[...] = (o / den).astype(o_ref.dtype)


def q_proj_cross_attention(xb, wq3, g3, rin, k, v, layer, seq, n_mem, name, tm=512, w_chunks=1):
    m, d = xb.shape
    head_dim = d // MEM_HEADS
    tm = min(tm, seq)
    tiles_per_batch = seq // tm
    last_slab = rin.shape[0] - 1
    w_chunks = _n_w_chunks(w_chunks, m // tm)
    kvspec = pl.BlockSpec((n_mem, head_dim), lambda j, i: (i // tiles_per_batch, j))
    return pl.pallas_call(
        functools.partial(_xq_attn_kernel, n_w=w_chunks),
        out_shape=jax.ShapeDtypeStruct((m, d), BF16),
        grid=(MEM_HEADS, m // tm),
        in_specs=[pl.BlockSpec((tm, d), lambda j, i: (i, 0))]
        + _w_chunk_specs(d, head_dim, layer, MEM_HEADS, w_chunks) + [
            pl.BlockSpec((None, d, V7X_LANES), lambda j, i: (layer, 0, 0)),
            pl.BlockSpec((None, tm, STAT_LANES), lambda j, i: (last_slab, i, 0)),
            kvspec, kvspec,
        ],
        out_specs=pl.BlockSpec((tm, head_dim), lambda j, i: (i, j)),
        scratch_shapes=[pltpu.VMEM((d, head_dim), BF16)],
        compiler_params=_cparams(("arbitrary", "arbitrary")),
        name=name,
    )(xb, *([wq3] * w_chunks), g3, rin, k, v)


RG_PAD = V7X_SUBLANES


def _rglru_kernel(x_ref, gate_ref, cw_ref, cb_ref, w_ref, ba_ref, bx_ref, lam_ref, y_ref,
                  pad_ref, af_ref, uf_ref, ab_ref, ub_ref, *, seq, heads):
    cw = heads * RG_BLOCK
    zeros = jnp.zeros((RG_PAD, cw), F32)
    pad_ref[0:RG_PAD, :] = zeros
    pad_ref[RG_PAD + seq:RG_PAD + seq + RG_PAD, :] = zeros
    pad_ref[RG_PAD:RG_PAD + seq, :] = x_ref[...]
    left = RG_CONV_W // 2
    xc = cb_ref[...] + cw_ref[0:1, :] * pad_ref[pl.ds(RG_PAD - left, seq), :]
    for j in range(1, RG_CONV_W):
        xc = xc + cw_ref[j:j + 1, :] * pad_ref[pl.ds(RG_PAD - left + j, seq), :]
    xcb = xc.astype(BF16)

    lam = lam_ref[...]
    nl = -lam
    softplus = jnp.maximum(nl, 0.0) + jnp.log1p(jnp.exp(-jnp.abs(nl)))
    ba = ba_ref[...]
    bx = bx_ref[...]
    for h in range(heads):
        c0 = h * RG_BLOCK
        sl = slice(c0, c0 + RG_BLOCK)
        gates = jnp.dot(xcb[:, sl], w_ref[h].astype(BF16), preferred_element_type=F32)
        xch = xc[:, sl]
        for d, (a_ref, u_ref) in enumerate(((af_ref, uf_ref), (ab_ref, ub_ref))):
            g0 = 2 * d * RG_BLOCK
            r = _sigmoid(gates[:, g0:g0 + RG_BLOCK] + ba[d:d + 1, sl])
            i = _sigmoid(gates[:, g0 + RG_BLOCK:g0 + 2 * RG_BLOCK] + bx[d:d + 1, sl])
            log_a = (-RG_C) * r * softplus[d:d + 1, sl]
            t = jnp.tanh(log_a)
            one_minus_a2 = (-2.0 * t) / (1.0 - t)
            a_ref[:, sl] = jnp.exp(log_a)
            u_ref[:, sl] = jnp.sqrt(one_minus_a2) * (i * xch)

    n_tiles = seq // V7X_SUBLANES
    row = lax.broadcasted_iota(jnp.int32, (V7X_SUBLANES, cw), 0)

    def tile_scan(a, u, reverse):
        for d in (1, 2, 4):
            if reverse:
                shift, keep = V7X_SUBLANES - d, row < V7X_SUBLANES - d
            else:
                shift, keep = d, row >= d
            a_s = pltpu.roll(a, shift, axis=0)
            u_s = pltpu.roll(u, shift, axis=0)
            u = jnp.where(keep, u + a * u_s, u)
            a = jnp.where(keep, a * a_s, a)
        return a, u

    def body(t, carry):
        hf, hb = carry
        rf = pl.multiple_of(t * V7X_SUBLANES, V7X_SUBLANES)
        a, u = tile_scan(af_ref[pl.ds(rf, V7X_SUBLANES), :], uf_ref[pl.ds(rf, V7X_SUBLANES), :], False)
        hh = a * hf + u
        uf_ref[pl.ds(rf, V7X_SUBLANES), :] = hh
        hf = jnp.broadcast_to(hh[V7X_SUBLANES - 1:V7X_SUBLANES, :], hh.shape)
        rb = pl.multiple_of((n_tiles - 1 - t) * V7X_SUBLANES, V7X_SUBLANES)
        a, u = tile_scan(ab_ref[pl.ds(rb, V7X_SUBLANES), :], ub_ref[pl.ds(rb, V7X_SUBLANES), :], True)
        hh = a * hb + u
        ub_ref[pl.ds(rb, V7X_SUBLANES), :] = hh
        hb = jnp.broadcast_to(hh[0:1, :], hh.shape)
        return hf, hb

    h0 = jnp.zeros((V7X_SUBLANES, cw), F32)
    lax.fori_loop(0, n_tiles, body, (h0, h0))

    g = gate_ref[...]
    gelu = 0.5 * g * (1.0 + jnp.tanh(math.sqrt(2.0 / math.pi) * (g + 0.044715 * (g * g * g))))
    y_ref[...] = ((uf_ref[...] + ub_ref[...]) * gelu).astype(y_ref.dtype)


def rglru(z3, cw3, cb3, w4, ba3, bx3, lam3, layer, name, heads=2):
    b, seq, _ = z3.shape
    cw = heads * RG_BLOCK
    ncb = RG_WIDTH // cw
    n_rg_heads = RG_WIDTH // RG_BLOCK
    kern = functools.partial(_rglru_kernel, seq=seq, heads=heads)
    vec = lambda rows: pl.BlockSpec((None, rows, cw), lambda bi, j: (layer, 0, j))
    return pl.pallas_call(
        kern,
        out_shape=jax.ShapeDtypeStruct((b, seq, RG_WIDTH), F32),
        grid=(b, ncb),
        in_specs=[
            pl.BlockSpec((None, seq, cw), lambda bi, j: (bi, 0, COL_RX * ncb + j)),
            pl.BlockSpec((None, seq, cw), lambda bi, j: (bi, 0, COL_RG * ncb + j)),
            vec(RG_CONV_W), vec(1),
            pl.BlockSpec((heads, RG_BLOCK, 4 * RG_BLOCK), lambda bi, j: ((layer * n_rg_heads) // heads + j, 0, 0)),
            vec(2), vec(2), vec(2),
        ],
        out_specs=pl.BlockSpec((None, seq, cw), lambda bi, j: (bi, 0, j)),
        scratch_shapes=[pltpu.VMEM((seq + 2 * RG_PAD, cw), F32)] + [pltpu.VMEM((seq, cw), F32)] * 4,
        compiler_params=_cparams(("arbitrary", "arbitrary")),
        name=name,
    )(z3, z3, cw3, cb3, w4, ba3, bx3, lam3)


CV_CHUNK = 32


def _conformer_kernel(a_ref, g_ref, ap_ref, gp_ref, an_ref, gn_ref, dw_ref, db_ref, lg_ref, lb_ref,
                      pw_ref, pb_ref, gg_ref, o_ref, buf_ref, sh_ref, conv_ref, pwb_ref, dwb_ref, *, ts):
    i = pl.program_id(1)
    n_i = pl.num_programs(1)

    @pl.when(jnp.logical_and(pl.program_id(0) == 0, i == 0))
    def _():
        pwb_ref[...] = pw_ref[...].astype(BF16)
        for j in range(CV_KERNEL):
            dwb_ref[j] = jnp.broadcast_to(dw_ref[j:j + 1, :], (V7X_SUBLANES, CV_WIDTH))

    def glu(a, g):
        return a * _sigmoid(g)

    prev = glu(ap_ref[...], gp_ref[...])
    nxt = glu(an_ref[...], gn_ref[...])
    buf_ref[0:CV_HALO, :] = jnp.where(i > 0, prev, 0.0)
    buf_ref[CV_HALO:CV_HALO + ts, :] = glu(a_ref[...], g_ref[...])
    buf_ref[CV_HALO + ts:CV_HALO + ts + CV_HALO, :] = jnp.where(i < n_i - 1, nxt, 0.0)

    span = ts + 2 * CV_HALO - V7X_SUBLANES
    for r in range(V7X_SUBLANES):
        sh_ref[r] = buf_ref[pl.ds(r, span), :]

    base_off = CV_HALO - CV_KERNEL // 2
    n_sub = CV_CHUNK // V7X_SUBLANES
    bias = jnp.broadcast_to(db_ref[...], (V7X_SUBLANES, CV_WIDTH))

    def chunk(c, carry):
        r0 = pl.multiple_of(c * CV_CHUNK, CV_CHUNK)
        accs = [bias] * n_sub
        for j in range(CV_KERNEL):
            off = base_off + j
            q, r = off // V7X_SUBLANES, off % V7X_SUBLANES
            w = dwb_ref[j]
            for s in range(n_sub):
                rows = pl.multiple_of(r0 + (q + s) * V7X_SUBLANES, V7X_SUBLANES)
                accs[s] = accs[s] + w * sh_ref[r, pl.ds(rows, V7X_SUBLANES), :]
        for s in range(n_sub):
            rows = pl.multiple_of(r0 + s * V7X_SUBLANES, V7X_SUBLANES)
            conv_ref[pl.ds(rows, V7X_SUBLANES), :] = accs[s]
        return carry

    lax.fori_loop(0, ts // CV_CHUNK, chunk, 0)

    y = conv_ref[...]
    mu = jnp.mean(y, axis=-1, keepdims=True)
    yc = y - mu
    var = jnp.mean(yc * yc, axis=-1, keepdims=True)
    yn = yc * lax.rsqrt(var + EPS) * lg_ref[...] + lb_ref[...]
    act = yn * _sigmoid(yn)
    out = jnp.dot(act.astype(BF16), pwb_ref[...], preferred_element_type=F32) + pb_ref[...]
    ms = jnp.mean(out * out, axis=-1, keepdims=True)
    o_ref[...] = (out * lax.rsqrt(ms + EPS) * gg_ref[...]).astype(o_ref.dtype)


def conformer(z3, dw3, db3, lg3, lb3, pw3, pb3, gg3, layer, name, ts=256):
    b, seq, _ = z3.shape
    ts = min(ts, seq)
    nt = seq // ts
    hb = ts // CV_HALO
    n_hblk = seq // CV_HALO
    main = lambda col: pl.BlockSpec((None, ts, CV_WIDTH), lambda bi, i: (bi, i, col))
    prev = lambda col: pl.BlockSpec((None, CV_HALO, CV_WIDTH),
                                    lambda bi, i: (bi, jnp.maximum(i * hb - 1, 0), col))
    nxt = lambda col: pl.BlockSpec((None, CV_HALO, CV_WIDTH),
                                   lambda bi, i: (bi, jnp.minimum((i + 1) * hb, n_hblk - 1), col))
    vec = lambda rows: pl.BlockSpec((None, rows, CV_WIDTH), lambda bi, i: (layer, 0, 0))
    kern = functools.partial(_conformer_kernel, ts=ts)
    return pl.pallas_call(
        kern,
        out_shape=jax.ShapeDtypeStruct((b, seq, CV_WIDTH), BF16),
        grid=(b, nt),
        in_specs=[
            main(COL_CA), main(COL_CB), prev(COL_CA), prev(COL_CB), nxt(COL_CA), nxt(COL_CB),
            vec(CV_KERNEL), vec(1), vec(1), vec(1),
            pl.BlockSpec((None, CV_WIDTH, CV_WIDTH), lambda bi, i: (layer, 0, 0)),
            vec(1),
            pl.BlockSpec((None, 1, CV_WIDTH), lambda bi, i: (layer, 0, (ATTN_WIDTH + RG_WIDTH) // CV_WIDTH)),
        ],
        out_specs=pl.BlockSpec((None, ts, CV_WIDTH), lambda bi, i: (bi, i, 0)),
        scratch_shapes=[
            pltpu.VMEM((ts + 2 * CV_HALO, CV_WIDTH), F32),
            pltpu.VMEM((V7X_SUBLANES, ts + 2 * CV_HALO - V7X_SUBLANES, CV_WIDTH), F32),
            pltpu.VMEM((ts, CV_WIDTH), F32),
            pltpu.VMEM((CV_WIDTH, CV_WIDTH), BF16),
            pltpu.VMEM((CV_KERNEL, V7X_SUBLANES, CV_WIDTH), F32),
        ],
        compiler_params=_cparams(("arbitrary", "arbitrary")),
        name=name,
    )(z3, z3, z3, z3, z3, z3, dw3, db3, lg3, lb3, pw3, pb3, gg3)


def kernel(x, mem, g_mix, w_in, g_q, g_k, rg_conv_w, rg_conv_b, rg_w_a, rg_b_a, rg_w_x, rg_b_x, rg_lam, cv_dw_w, cv_dw_b, cv_ln_g, cv_ln_b, cv_pw_w, cv_pw_b, g_grp, w_out, g_xattn, g_mem, xa_wq, xa_wk, xa_wv, xa_wo, g_ffn, ffn_wg, ffn_wu, ffn_wd, g_final):
    b, seq, d = x.shape
    n_mem = mem.shape[1]
    depth = w_in.shape[0]
    m = b * seq
    row3 = lambda p: p.reshape(p.shape[0], 1, p.shape[-1])

    cos_t, sin_t = rope_tables(seq)
    g_q3, g_k3, g_grp3, g_mem3 = row3(g_q), row3(g_k), row3(g_grp), row3(g_mem)
    g_final3 = g_final.reshape(1, 1, d)
    rg_cb3, cv_db3, cv_lg3, cv_lb3, cv_pb3 = (row3(rg_conv_b), row3(cv_dw_b), row3(cv_ln_g),
                                              row3(cv_ln_b), row3(cv_pw_b))
    rg_w4 = jnp.concatenate([rg_w_a[:, 0], rg_w_x[:, 0], rg_w_a[:, 1], rg_w_x[:, 1]], axis=-1)
    rg_w4 = rg_w4.reshape(-1, RG_BLOCK, 4 * RG_BLOCK)

    col3 = lambda p: jnp.broadcast_to(p[:, :, None], (p.shape[0], p.shape[1], V7X_LANES))
    g_mix_c, g_xattn_c, g_ffn_c = col3(g_mix), col3(g_xattn), col3(g_ffn)
    wd_bf16 = cast_bf16(ffn_wd, "cast_wd")

    xf = x.reshape(m, d)
    memf = mem.reshape(b * n_mem, d)
    xb, xr = row_stats(xf, "x_stats")

    for l in range(depth):
        last = l == depth - 1
        z = matmul([xb], w_in, l, F32, f"in_proj{l}", tn=1024, tm=512, norm_in=(g_mix_c, xr),
                   w_chunks=W_PREFETCH_CHUNKS)
        z3 = z.reshape(b, seq, z.shape[1])
        y_attn = attention(z, g_q3, g_k3, cos_t, sin_t, l, b, seq, f"attn{l}")
        y_rec = rglru(z3, rg_conv_w, rg_cb3, rg_w4, rg_b_a, rg_b_x, rg_lam, l, f"rglru{l}")
        y_cv_n = conformer(z3, cv_dw_w, cv_db3, cv_lg3, cv_lb3, cv_pw_w, cv_pb3, g_grp3, l, f"conformer{l}")
        y_attn_n = rmsnorm(y_attn, g_grp3, l, 0, BF16, f"norm_attn{l}")
        y_rec_n = rmsnorm(y_rec.reshape(m, RG_WIDTH), g_grp3, l, ATTN_WIDTH // RG_WIDTH, BF16, f"norm_rec{l}")
        xf, xb, xr = matmul([y_attn_n, y_rec_n, y_cv_n.reshape(m, CV_WIDTH)], w_out, l, F32, f"out_proj{l}",
                            tn=1024, tm=512, res=xf, emit_norm=True, single_buffer_w=True)

        mn = rmsnorm(memf, g_mem3, l, 0, BF16, f"norm_mem{l}")
        k2 = matmul([mn], xa_wk, l, BF16, f"xa_k{l}", tn=512, tm=1024)
        v2 = matmul([mn], xa_wv, l, BF16, f"xa_v{l}", tn=512, tm=1024)
        o2 = q_proj_cross_attention(xb, xa_wq, g_xattn_c, xr, k2, v2, l, seq, n_mem, f"xa_q_attn{l}",
                                    w_chunks=W_PREFETCH_CHUNKS)
        xf, xb, xr = matmul([o2], xa_wo, l, F32, f"xa_o{l}", tn=1024, tm=512, res=xf, emit_norm=True,
                            single_buffer_w=True)

        act = swiglu(xb, ffn_wg, ffn_wu, g_ffn_c, xr, l, f"ffn_up{l}", tn=256, tm=1024,
                     w_chunks=W_PREFETCH_CHUNKS)
        if last:
            xf = matmul([act], wd_bf16, l, F32, f"ffn_down{l}", tn=512, tm=512, res=xf)
        else:
            xf, xb, xr = matmul([act], wd_bf16, l, F32, f"ffn_down{l}", tn=512, tm=512, res=xf, emit_norm=True)

    out = rmsnorm(xf, g_final3, 0, 0, F32, "norm_final")
    return out.reshape(b, seq, d)
```

```python
import functools
import math

import jax
import jax.numpy as jnp
from jax import lax
from jax.experimental import pallas as pl
from jax.experimental.pallas import tpu as pltpu

F32 = jnp.float32
BF16 = jnp.bfloat16

HEAD_DIM = 128
N_Q_HEADS = 16
N_KV_HEADS = 4
Q_PER_KV = N_Q_HEADS // N_KV_HEADS
ATTN_WIDTH = N_Q_HEADS * HEAD_DIM
KV_WIDTH = N_KV_HEADS * HEAD_DIM
RG_WIDTH = 1024
RG_BLOCK = 128
RG_CONV_W = 4
RG_C = 8.0
CV_WIDTH = 1024
CV_KERNEL = 31
QKV_WIDTH = ATTN_WIDTH + 2 * KV_WIDTH
COL_RX = QKV_WIDTH // RG_WIDTH
COL_RG = COL_RX + 1
COL_CA = COL_RG + 1
COL_CB = COL_CA + 1
MEM_HEADS = 4
GRID_W = 64
ROPE_THETA = 10000.0
ROPE_AXIS_DIM = HEAD_DIM // 2
EPS = 1e-6

V7X_VMEM_BYTES = 64 * 1024 * 1024
V7X_SUBLANES = 8
V7X_LANES = 128
VMEM_LIMIT = V7X_VMEM_BYTES - 4 * 1024 * 1024

CV_HALO = 16


def _sigmoid(x):
    return 0.5 * jnp.tanh(0.5 * x) + 0.5


def _cparams(sem):
    return pltpu.CompilerParams(dimension_semantics=sem, vmem_limit_bytes=VMEM_LIMIT)


def _rmsnorm_kernel(x_ref, g_ref, o_ref):
    x = x_ref[...].astype(F32)
    ms = jnp.mean(x * x, axis=-1, keepdims=True)
    o_ref[...] = (x * lax.rsqrt(ms + EPS) * g_ref[...]).astype(o_ref.dtype)


def rmsnorm(x, g3, layer, gcol, out_dtype, name, tm=512):
    m, w = x.shape
    tm = min(tm, m)
    return pl.pallas_call(
        _rmsnorm_kernel,
        out_shape=jax.ShapeDtypeStruct((m, w), out_dtype),
        grid=(m // tm,),
        in_specs=[
            pl.BlockSpec((tm, w), lambda i: (i, 0)),
            pl.BlockSpec((None, 1, w), lambda i: (layer, 0, gcol)),
        ],
        out_specs=pl.BlockSpec((tm, w), lambda i: (i, 0)),
        compiler_params=_cparams(("arbitrary",)),
        name=name,
    )(x, g3)


CAST_ROWS = 256
W_PREFETCH_CHUNKS = 4

TILE_IN_PROJ = (512, 1024)
TILE_OUT_PROJ = (512, 1024)
TILE_XA_KV = (1024, 512)
TILE_XA_Q = 512
TILE_XA_O = (512, 1024)
TILE_FFN_UP = (1024, 256)
TILE_FFN_DOWN = (512, 512)


def _cast_panel(w_refs, wb_ref, g_ref=None):
    kc, tn = w_refs[0].shape

    for ci, w_ref in enumerate(w_refs):
        def body(i, c, w_ref=w_ref, base=ci * kc):
            src = pl.ds(pl.multiple_of(i * CAST_ROWS, CAST_ROWS), CAST_ROWS)
            dst = pl.ds(pl.multiple_of(base + i * CAST_ROWS, CAST_ROWS), CAST_ROWS)
            if g_ref is None:
                wb_ref[dst, :] = w_ref[src, :].astype(BF16)
            else:
                g = g_ref[dst, :]
                for c0 in range(0, tn, V7X_LANES):
                    wb_ref[dst, c0:c0 + V7X_LANES] = (w_ref[src, c0:c0 + V7X_LANES] * g).astype(BF16)
            return c

        lax.fori_loop(0, kc // CAST_ROWS, body, 0)


def _n_w_chunks(requested, n_row_tiles):
    limit = max(1, min(requested, n_row_tiles - 1))
    return 1 << (limit.bit_length() - 1)


def _w_chunk_specs(k, tn, layer, n_panels, n_chunks):
    kc = k // n_chunks
    assert kc * n_chunks == k and kc % CAST_ROWS == 0

    def spec(c):
        def index(j, i):
            return layer, c, jnp.where(i <= c, j, jnp.minimum(j + 1, n_panels - 1))
        return pl.BlockSpec((None, kc, tn), index)

    return [spec(c) for c in range(n_chunks)]


STAT_LANES = V7X_LANES


def _mm_kernel(*refs, k_splits, n_w, has_norm_in, has_res, emit_norm, w_is_bf16, n_total):
    n_a = len(k_splits)
    a_refs = refs[:n_a]
    w_refs = refs[n_a:n_a + n_w]
    pos = n_a + n_w
    g_ref = rin_ref = res_ref = xb_ref = rout_ref = ssq_ref = None
    if has_norm_in:
        g_ref, rin_ref = refs[pos], refs[pos + 1]
        pos += 2
    if has_res:
        res_ref = refs[pos]
        pos += 1
    o_ref = refs[pos]
    pos += 1
    if emit_norm:
        xb_ref, rout_ref = refs[pos], refs[pos + 1]
        pos += 2
    j = pl.program_id(0)
    i = pl.program_id(1)
    if w_is_bf16:
        wb_ref = w_refs[0]
    else:
        wb_ref = refs[pos]
        pos += 1

        @pl.when(i == 0)
        def _():
            _cast_panel(w_refs, wb_ref, g_ref)

    acc = None
    off = 0
    for a_ref, kk in zip(a_refs, k_splits):
        d = jnp.dot(a_ref[...], wb_ref[off:off + kk, :], preferred_element_type=F32)
        acc = d if acc is None else acc + d
        off += kk
    if has_norm_in:
        acc = acc * rin_ref[:, 0:1]
    if has_res:
        acc = acc + res_ref[...]
    o_ref[...] = acc.astype(o_ref.dtype)
    if emit_norm:
        ssq_ref = refs[pos]
        tm = acc.shape[0]
        xb_ref[...] = acc.astype(BF16)
        rows = pl.ds(pl.multiple_of(i * tm, tm), tm)
        part = jnp.broadcast_to(jnp.sum(acc * acc, axis=-1, keepdims=True), (tm, STAT_LANES))

        @pl.when(j == 0)
        def _():
            ssq_ref[rows, :] = part

        @pl.when(j > 0)
        def _():
            ssq_ref[rows, :] = ssq_ref[rows, :] + part

        rout_ref[...] = lax.rsqrt(ssq_ref[rows, :] * (1.0 / n_total) + EPS)


def matmul(a_list, w3, layer, out_dtype, name, *, tn, tm, norm_in=None, res=None, emit_norm=False,
           single_buffer_w=False, w_chunks=1):
    m = a_list[0].shape[0]
    k_splits = tuple(a.shape[1] for a in a_list)
    _, k, n = w3.shape
    assert sum(k_splits) == k
    tm = min(tm, m)
    w_is_bf16 = w3.dtype == BF16
    assert not (w_is_bf16 and norm_in is not None)
    in_specs = [pl.BlockSpec((tm, kk), lambda j, i: (i, 0)) for kk in k_splits]
    assert w_chunks == 1 or not (w_is_bf16 or single_buffer_w)
    w_chunks = _n_w_chunks(w_chunks, m // tm)
    if w_chunks > 1:
        in_specs += _w_chunk_specs(k, tn, layer, n // tn, w_chunks)
    else:
        w_mode = dict(pipeline_mode=pl.Buffered(1)) if single_buffer_w else {}
        in_specs.append(pl.BlockSpec((None, k, tn), lambda j, i: (layer, 0, j), **w_mode))
    args = list(a_list) + [w3] * w_chunks
    if norm_in is not None:
        last_slab = norm_in[1].shape[0] - 1
        in_specs.append(pl.BlockSpec((None, k, V7X_LANES), lambda j, i: (layer, 0, 0)))
        in_specs.append(pl.BlockSpec((None, tm, STAT_LANES), lambda j, i: (last_slab, i, 0)))
        args += list(norm_in)
    if res is not None:
        in_specs.append(pl.BlockSpec((tm, tn), lambda j, i: (i, j)))
        args.append(res)
    out_shape = [jax.ShapeDtypeStruct((m, n), out_dtype)]
    out_specs = [pl.BlockSpec((tm, tn), lambda j, i: (i, j))]
    scratch = [] if w_is_bf16 else [pltpu.VMEM((k, tn), BF16)]
    if emit_norm:
        out_shape += [jax.ShapeDtypeStruct((m, n), BF16), jax.ShapeDtypeStruct((n // tn, m, STAT_LANES), F32)]
        out_specs += [pl.BlockSpec((tm, tn), lambda j, i: (i, j)),
                      pl.BlockSpec((None, tm, STAT_LANES), lambda j, i: (j, i, 0))]
        scratch.append(pltpu.VMEM((m, STAT_LANES), F32))
    kern = functools.partial(_mm_kernel, k_splits=k_splits, n_w=w_chunks, has_norm_in=norm_in is not None,
                             has_res=res is not None, emit_norm=emit_norm, w_is_bf16=w_is_bf16, n_total=n)
    out = pl.pallas_call(
        kern,
        out_shape=tuple(out_shape),
        grid=(n // tn, m // tm),
        in_specs=in_specs,
        out_specs=tuple(out_specs),
        scratch_shapes=scratch,
        compiler_params=_cparams(("arbitrary", "arbitrary")),
        name=name,
    )(*args)
    return out if emit_norm else out[0]


def _row_stats_kernel(x_ref, xb_ref, r_ref):
    x = x_ref[...]
    xb_ref[...] = x.astype(BF16)
    ms = jnp.mean(x * x, axis=-1, keepdims=True)
    r_ref[...] = jnp.broadcast_to(lax.rsqrt(ms + EPS), r_ref.shape)


def row_stats(x, name, tm=512):
    m, w = x.shape
    tm = min(tm, m)
    return pl.pallas_call(
        _row_stats_kernel,
        out_shape=(jax.ShapeDtypeStruct((m, w), BF16), jax.ShapeDtypeStruct((1, m, STAT_LANES), F32)),
        grid=(m // tm,),
        in_specs=[pl.BlockSpec((tm, w), lambda i: (i, 0))],
        out_specs=(pl.BlockSpec((tm, w), lambda i: (i, 0)),
                   pl.BlockSpec((None, tm, STAT_LANES), lambda i: (0, i, 0))),
        compiler_params=_cparams(("arbitrary",)),
        name=name,
    )(x)


def _swiglu_kernel(*refs, n_w):
    a_ref = refs[0]
    wg_refs = refs[1:1 + n_w]
    wu_refs = refs[1 + n_w:1 + 2 * n_w]
    g_ref, rin_ref, o_ref, wgb_ref, wub_ref = refs[1 + 2 * n_w:]

    @pl.when(pl.program_id(1) == 0)
    def _():
        _cast_panel(wg_refs, wgb_ref, g_ref)
        _cast_panel(wu_refs, wub_ref, g_ref)

    a = a_ref[...]
    r = rin_ref[:, 0:1]
    g = jnp.dot(a, wgb_ref[...], preferred_element_type=F32) * r
    u = jnp.dot(a, wub_ref[...], preferred_element_type=F32) * r
    o_ref[...] = (g * _sigmoid(g) * u).astype(o_ref.dtype)


def swiglu(a, wg3, wu3, g3, rin, layer, name, *, tn, tm, w_chunks):
    m, k = a.shape
    n = wg3.shape[2]
    tm = min(tm, m)
    assert n % tn == 0
    w_chunks = _n_w_chunks(w_chunks, m // tm)
    wspecs = _w_chunk_specs(k, tn, layer, n // tn, w_chunks)
    last_slab = rin.shape[0] - 1
    return pl.pallas_call(
        functools.partial(_swiglu_kernel, n_w=w_chunks),
        out_shape=jax.ShapeDtypeStruct((m, n), BF16),
        grid=(n // tn, m // tm),
        in_specs=[pl.BlockSpec((tm, k), lambda j, i: (i, 0))] + wspecs + wspecs + [
            pl.BlockSpec((None, k, V7X_LANES), lambda j, i: (layer, 0, 0)),
            pl.BlockSpec((None, tm, STAT_LANES), lambda j, i: (last_slab, i, 0))],
        out_specs=pl.BlockSpec((tm, tn), lambda j, i: (i, j)),
        scratch_shapes=[pltpu.VMEM((k, tn), BF16), pltpu.VMEM((k, tn), BF16)],
        compiler_params=_cparams(("arbitrary", "arbitrary")),
        name=name,
    )(a, *([wg3] * w_chunks), *([wu3] * w_chunks), g3, rin)


def _cast_kernel(x_ref, o_ref):
    o_ref[...] = x_ref[...].astype(o_ref.dtype)


def cast_bf16(w3, name, rows=256):
    l, k, n = w3.shape
    return pl.pallas_call(
        _cast_kernel,
        out_shape=jax.ShapeDtypeStruct((l, k, n), BF16),
        grid=(l, k // rows),
        in_specs=[pl.BlockSpec((None, rows, n), lambda a, i: (a, i, 0))],
        out_specs=pl.BlockSpec((None, rows, n), lambda a, i: (a, i, 0)),
        compiler_params=_cparams(("arbitrary", "arbitrary")),
        name=name,
    )(w3)


def rope_tables(seq):
    rows = seq // GRID_W
    row = jnp.repeat(jnp.arange(rows), GRID_W).astype(F32)
    col = jnp.tile(jnp.arange(GRID_W), rows).astype(F32)
    inv = ROPE_THETA ** (-jnp.arange(0, ROPE_AXIS_DIM, 2, dtype=F32) / ROPE_AXIS_DIM)
    ang_r = row[:, None] * inv
    ang_c = col[:, None] * inv
    cos_t = jnp.concatenate([jnp.cos(ang_r), jnp.cos(ang_r), jnp.cos(ang_c), jnp.cos(ang_c)], axis=1)
    sin_t = jnp.concatenate([-jnp.sin(ang_r), jnp.sin(ang_r), -jnp.sin(ang_c), jnp.sin(ang_c)], axis=1)
    return cos_t, sin_t


def _norm_rope(x, g, cos, sin):
    half = ROPE_AXIS_DIM // 2
    lane = lax.broadcasted_iota(jnp.int32, (1, HEAD_DIM), 1)
    first_half = (lane % ROPE_AXIS_DIM) < half
    ms = jnp.mean(x * x, axis=-1, keepdims=True)
    xn = x * lax.rsqrt(ms + EPS) * g
    swapped = jnp.where(first_half, pltpu.roll(xn, HEAD_DIM - half, axis=1), pltpu.roll(xn, half, axis=1))
    return xn * cos + swapped * sin


def _attn_kernel(zq_ref, zk_ref, zv_ref, gq_ref, gk_ref, cos_ref, sin_ref, o_ref, k_s, v_s, *, tq, seq):
    i = pl.program_id(2)

    @pl.when(i == 0)
    def _():
        k_s[...] = _norm_rope(zk_ref[...], gk_ref[...], cos_ref[...], sin_ref[...]).astype(BF16)
        v_s[:, 0:HEAD_DIM] = zv_ref[...].astype(BF16)
        v_s[:, HEAD_DIM:2 * HEAD_DIM] = jnp.ones((seq, HEAD_DIM), BF16)

    r0 = pl.multiple_of(i * tq, tq)
    cos = cos_ref[pl.ds(r0, tq), :]
    sin = sin_ref[pl.ds(r0, tq), :]
    gq = gq_ref[...]
    k = k_s[...]
    v = v_s[...]
    c = (HEAD_DIM ** -0.5) * math.log2(math.e)
    for h in range(Q_PER_KV):
        c0 = h * HEAD_DIM
        q = _norm_rope(zq_ref[:, c0:c0 + HEAD_DIM], gq, cos, sin).astype(BF16)
        s = lax.dot_general(q, k, (((1,), (1,)), ((), ())), preferred_element_type=F32)
        mx = jnp.max(s, axis=-1, keepdims=True)
        p = jnp.exp2((s - mx) * c).astype(BF16)
        o = jnp.dot(p, v, preferred_element_type=F32)
        o_ref[:, c0:c0 + HEAD_DIM] = (o[:, 0:HEAD_DIM] / o[:, HEAD_DIM:2 * HEAD_DIM]).astype(o_ref.dtype)


def attention(z, gq3, gk3, cos_t, sin_t, layer, batch, seq, name, tq=256):
    m = z.shape[0]
    tq = min(tq, seq)
    nq = seq // tq
    gw = Q_PER_KV * HEAD_DIM
    kcol = ATTN_WIDTH // HEAD_DIM
    vcol = (ATTN_WIDTH + KV_WIDTH) // HEAD_DIM
    gspec = pl.BlockSpec((None, 1, HEAD_DIM), lambda b, g, i: (layer, 0, 0))
    tspec = pl.BlockSpec((seq, HEAD_DIM), lambda b, g, i: (0, 0))
    kern = functools.partial(_attn_kernel, tq=tq, seq=seq)
    return pl.pallas_call(
        kern,
        out_shape=jax.ShapeDtypeStruct((m, ATTN_WIDTH), F32),
        grid=(batch, N_KV_HEADS, nq),
        in_specs=[
            pl.BlockSpec((tq, gw), lambda b, g, i: (b * nq + i, g)),
            pl.BlockSpec((seq, HEAD_DIM), lambda b, g, i: (b, kcol + g)),
            pl.BlockSpec((seq, HEAD_DIM), lambda b, g, i: (b, vcol + g)),
            gspec, gspec, tspec, tspec,
        ],
        out_specs=pl.BlockSpec((tq, gw), lambda b, g, i: (b * nq + i, g)),
        scratch_shapes=[pltpu.VMEM((seq, HEAD_DIM), BF16), pltpu.VMEM((seq, 2 * HEAD_DIM), BF16)],
        compiler_params=_cparams(("arbitrary", "arbitrary", "arbitrary")),
        name=name,
    )(z, z, z, gq3, gk3, cos_t, sin_t)


def _xq_attn_kernel(*refs, n_w):
    x_ref = refs[0]
    w_refs = refs[1:1 + n_w]
    g_ref, rin_ref, k_ref, v_ref, o_ref, wb_ref = refs[1 + n_w:]

    @pl.when(pl.program_id(1) == 0)
    def _():
        _cast_panel(w_refs, wb_ref, g_ref)

    head_dim = wb_ref.shape[1]
    q = jnp.dot(x_ref[...], wb_ref[...], preferred_element_type=F32) * rin_ref[:, 0:1]
    k = k_ref[...]
    v = v_ref[...]
    s = lax.dot_general(q.astype(BF16), k, (((1,), (1,)), ((), ())), preferred_element_type=F32)
    s = s * (head_dim ** -0.5)
    mx = jnp.max(s, axis=-1, keepdims=True)
    p = jnp.exp(s - mx)
    den = jnp.sum(p, axis=-1, keepdims=True)
    o = jnp.dot(p.astype(BF16), v, preferred_element_type=F32)
    o_ref[...] = (o / den).astype(o_ref.dtype)


def q_proj_cross_attention(xb, wq3, g3, rin, k, v, layer, seq, n_mem, name, tm=512, w_chunks=1):
    m, d = xb.shape
    head_dim = d // MEM_HEADS
    tm = min(tm, seq)
    tiles_per_batch = seq // tm
    last_slab = rin.shape[0] - 1
    w_chunks = _n_w_chunks(w_chunks, m // tm)
    kvspec = pl.BlockSpec((n_mem, head_dim), lambda j, i: (i // tiles_per_batch, j))
    return pl.pallas_call(
        functools.partial(_xq_attn_kernel, n_w=w_chunks),
        out_shape=jax.ShapeDtypeStruct((m, d), BF16),
        grid=(MEM_HEADS, m // tm),
        in_specs=[pl.BlockSpec((tm, d), lambda j, i: (i, 0))]
        + _w_chunk_specs(d, head_dim, layer, MEM_HEADS, w_chunks) + [
            pl.BlockSpec((None, d, V7X_LANES), lambda j, i: (layer, 0, 0)),
            pl.BlockSpec((None, tm, STAT_LANES), lambda j, i: (last_slab, i, 0)),
            kvspec, kvspec,
        ],
        out_specs=pl.BlockSpec((tm, head_dim), lambda j, i: (i, j)),
        scratch_shapes=[pltpu.VMEM((d, head_dim), BF16)],
        compiler_params=_cparams(("arbitrary", "arbitrary")),
        name=name,
    )(xb, *([wq3] * w_chunks), g3, rin, k, v)


RG_PAD = V7X_SUBLANES


def _rglru_kernel(x_ref, gate_ref, cw_ref, cb_ref, w_ref, ba_ref, bx_ref, lam_ref, y_ref,
                  pad_ref, af_ref, uf_ref, ab_ref, ub_ref, *, seq, heads):
    cw = heads * RG_BLOCK
    zeros = jnp.zeros((RG_PAD, cw), F32)
    pad_ref[0:RG_PAD, :] = zeros
    pad_ref[RG_PAD + seq:RG_PAD + seq + RG_PAD, :] = zeros
    pad_ref[RG_PAD:RG_PAD + seq, :] = x_ref[...]
    left = RG_CONV_W // 2
    xc = cb_ref[...] + cw_ref[0:1, :] * pad_ref[pl.ds(RG_PAD - left, seq), :]
    for j in range(1, RG_CONV_W):
        xc = xc + cw_ref[j:j + 1, :] * pad_ref[pl.ds(RG_PAD - left + j, seq), :]
    xcb = xc.astype(BF16)

    lam = lam_ref[...]
    nl = -lam
    softplus = jnp.maximum(nl, 0.0) + jnp.log1p(jnp.exp(-jnp.abs(nl)))
    ba = ba_ref[...]
    bx = bx_ref[...]
    for h in range(heads):
        c0 = h * RG_BLOCK
        sl = slice(c0, c0 + RG_BLOCK)
        gates = jnp.dot(xcb[:, sl], w_ref[h].astype(BF16), preferred_element_type=F32)
        xch = xc[:, sl]
        for d, (a_ref, u_ref) in enumerate(((af_ref, uf_ref), (ab_ref, ub_ref))):
            g0 = 2 * d * RG_BLOCK
            r = _sigmoid(gates[:, g0:g0 + RG_BLOCK] + ba[d:d + 1, sl])
            i = _sigmoid(gates[:, g0 + RG_BLOCK:g0 + 2 * RG_BLOCK] + bx[d:d + 1, sl])
            log_a = (-RG_C) * r * softplus[d:d + 1, sl]
            t = jnp.tanh(log_a)
            one_minus_a2 = (-2.0 * t) / (1.0 - t)
            a_ref[:, sl] = jnp.exp(log_a)
            u_ref[:, sl] = jnp.sqrt(one_minus_a2) * (i * xch)

    n_tiles = seq // V7X_SUBLANES
    row = lax.broadcasted_iota(jnp.int32, (V7X_SUBLANES, cw), 0)

    def tile_scan(a, u, reverse):
        for d in (1, 2, 4):
            if reverse:
                shift, keep = V7X_SUBLANES - d, row < V7X_SUBLANES - d
            else:
                shift, keep = d, row >= d
            a_s = pltpu.roll(a, shift, axis=0)
            u_s = pltpu.roll(u, shift, axis=0)
            u = jnp.where(keep, u + a * u_s, u)
            a = jnp.where(keep, a * a_s, a)
        return a, u

    def body(t, carry):
        hf, hb = carry
        rf = pl.multiple_of(t * V7X_SUBLANES, V7X_SUBLANES)
        a, u = tile_scan(af_ref[pl.ds(rf, V7X_SUBLANES), :], uf_ref[pl.ds(rf, V7X_SUBLANES), :], False)
        hh = a * hf + u
        uf_ref[pl.ds(rf, V7X_SUBLANES), :] = hh
        hf = jnp.broadcast_to(hh[V7X_SUBLANES - 1:V7X_SUBLANES, :], hh.shape)
        rb = pl.multiple_of((n_tiles - 1 - t) * V7X_SUBLANES, V7X_SUBLANES)
        a, u = tile_scan(ab_ref[pl.ds(rb, V7X_SUBLANES), :], ub_ref[pl.ds(rb, V7X_SUBLANES), :], True)
        hh = a * hb + u
        ub_ref[pl.ds(rb, V7X_SUBLANES), :] = hh
        hb = jnp.broadcast_to(hh[0:1, :], hh.shape)
        return hf, hb

    h0 = jnp.zeros((V7X_SUBLANES, cw), F32)
    lax.fori_loop(0, n_tiles, body, (h0, h0))

    g = gate_ref[...]
    gelu = 0.5 * g * (1.0 + jnp.tanh(math.sqrt(2.0 / math.pi) * (g + 0.044715 * (g * g * g))))
    y_ref[...] = ((uf_ref[...] + ub_ref[...]) * gelu).astype(y_ref.dtype)


def rglru(z3, cw3, cb3, w4, ba3, bx3, lam3, layer, name, heads=2):
    b, seq, _ = z3.shape
    cw = heads * RG_BLOCK
    ncb = RG_WIDTH // cw
    n_rg_heads = RG_WIDTH // RG_BLOCK
    kern = functools.partial(_rglru_kernel, seq=seq, heads=heads)
    vec = lambda rows: pl.BlockSpec((None, rows, cw), lambda bi, j: (layer, 0, j))
    return pl.pallas_call(
        kern,
        out_shape=jax.ShapeDtypeStruct((b, seq, RG_WIDTH), F32),
        grid=(b, ncb),
        in_specs=[
            pl.BlockSpec((None, seq, cw), lambda bi, j: (bi, 0, COL_RX * ncb + j)),
            pl.BlockSpec((None, seq, cw), lambda bi, j: (bi, 0, COL_RG * ncb + j)),
            vec(RG_CONV_W), vec(1),
            pl.BlockSpec((heads, RG_BLOCK, 4 * RG_BLOCK), lambda bi, j: ((layer * n_rg_heads) // heads + j, 0, 0)),
            vec(2), vec(2), vec(2),
        ],
        out_specs=pl.BlockSpec((None, seq, cw), lambda bi, j: (bi, 0, j)),
        scratch_shapes=[pltpu.VMEM((seq + 2 * RG_PAD, cw), F32)] + [pltpu.VMEM((seq, cw), F32)] * 4,
        compiler_params=_cparams(("arbitrary", "arbitrary")),
        name=name,
    )(z3, z3, cw3, cb3, w4, ba3, bx3, lam3)


CV_CHUNK = 32


def _conformer_kernel(a_ref, g_ref, ap_ref, gp_ref, an_ref, gn_ref, dw_ref, db_ref, lg_ref, lb_ref,
                      pw_ref, pb_ref, gg_ref, o_ref, buf_ref, sh_ref, conv_ref, pwb_ref, dwb_ref, *, ts):
    i = pl.program_id(1)
    n_i = pl.num_programs(1)

    @pl.when(jnp.logical_and(pl.program_id(0) == 0, i == 0))
    def _():
        pwb_ref[...] = pw_ref[...].astype(BF16)
        for j in range(CV_KERNEL):
            dwb_ref[j] = jnp.broadcast_to(dw_ref[j:j + 1, :], (V7X_SUBLANES, CV_WIDTH))

    def glu(a, g):
        return a * _sigmoid(g)

    prev = glu(ap_ref[...], gp_ref[...])
    nxt = glu(an_ref[...], gn_ref[...])
    buf_ref[0:CV_HALO, :] = jnp.where(i > 0, prev, 0.0)
    buf_ref[CV_HALO:CV_HALO + ts, :] = glu(a_ref[...], g_ref[...])
    buf_ref[CV_HALO + ts:CV_HALO + ts + CV_HALO, :] = jnp.where(i < n_i - 1, nxt, 0.0)

    span = ts + 2 * CV_HALO - V7X_SUBLANES
    for r in range(V7X_SUBLANES):
        sh_ref[r] = buf_ref[pl.ds(r, span), :]

    base_off = CV_HALO - CV_KERNEL // 2
    n_sub = CV_CHUNK // V7X_SUBLANES
    bias = jnp.broadcast_to(db_ref[...], (V7X_SUBLANES, CV_WIDTH))

    def chunk(c, carry):
        r0 = pl.multiple_of(c * CV_CHUNK, CV_CHUNK)
        accs = [bias] * n_sub
        for j in range(CV_KERNEL):
            off = base_off + j
            q, r = off // V7X_SUBLANES, off % V7X_SUBLANES
            w = dwb_ref[j]
            for s in range(n_sub):
                rows = pl.multiple_of(r0 + (q + s) * V7X_SUBLANES, V7X_SUBLANES)
                accs[s] = accs[s] + w * sh_ref[r, pl.ds(rows, V7X_SUBLANES), :]
        for s in range(n_sub):
            rows = pl.multiple_of(r0 + s * V7X_SUBLANES, V7X_SUBLANES)
            conv_ref[pl.ds(rows, V7X_SUBLANES), :] = accs[s]
        return carry

    lax.fori_loop(0, ts // CV_CHUNK, chunk, 0)

    y = conv_ref[...]
    mu = jnp.mean(y, axis=-1, keepdims=True)
    yc = y - mu
    var = jnp.mean(yc * yc, axis=-1, keepdims=True)
    yn = yc * lax.rsqrt(var + EPS) * lg_ref[...] + lb_ref[...]
    act = yn * _sigmoid(yn)
    out = jnp.dot(act.astype(BF16), pwb_ref[...], preferred_element_type=F32) + pb_ref[...]
    ms = jnp.mean(out * out, axis=-1, keepdims=True)
    o_ref[...] = (out * lax.rsqrt(ms + EPS) * gg_ref[...]).astype(o_ref.dtype)


def conformer(z3, dw3, db3, lg3, lb3, pw3, pb3, gg3, layer, name, ts=256):
    b, seq, _ = z3.shape
    ts = min(ts, seq)
    nt = seq // ts
    hb = ts // CV_HALO
    n_hblk = seq // CV_HALO
    main = lambda col: pl.BlockSpec((None, ts, CV_WIDTH), lambda bi, i: (bi, i, col))
    prev = lambda col: pl.BlockSpec((None, CV_HALO, CV_WIDTH),
                                    lambda bi, i: (bi, jnp.maximum(i * hb - 1, 0), col))
    nxt = lambda col: pl.BlockSpec((None, CV_HALO, CV_WIDTH),
                                   lambda bi, i: (bi, jnp.minimum((i + 1) * hb, n_hblk - 1), col))
    vec = lambda rows: pl.BlockSpec((None, rows, CV_WIDTH), lambda bi, i: (layer, 0, 0))
    kern = functools.partial(_conformer_kernel, ts=ts)
    return pl.pallas_call(
        kern,
        out_shape=jax.ShapeDtypeStruct((b, seq, CV_WIDTH), BF16),
        grid=(b, nt),
        in_specs=[
            main(COL_CA), main(COL_CB), prev(COL_CA), prev(COL_CB), nxt(COL_CA), nxt(COL_CB),
            vec(CV_KERNEL), vec(1), vec(1), vec(1),
            pl.BlockSpec((None, CV_WIDTH, CV_WIDTH), lambda bi, i: (layer, 0, 0)),
            vec(1),
            pl.BlockSpec((None, 1, CV_WIDTH), lambda bi, i: (layer, 0, (ATTN_WIDTH + RG_WIDTH) // CV_WIDTH)),
        ],
        out_specs=pl.BlockSpec((None, ts, CV_WIDTH), lambda bi, i: (bi, i, 0)),
        scratch_shapes=[
            pltpu.VMEM((ts + 2 * CV_HALO, CV_WIDTH), F32),
            pltpu.VMEM((V7X_SUBLANES, ts + 2 * CV_HALO - V7X_SUBLANES, CV_WIDTH), F32),
            pltpu.VMEM((ts, CV_WIDTH), F32),
            pltpu.VMEM((CV_WIDTH, CV_WIDTH), BF16),
            pltpu.VMEM((CV_KERNEL, V7X_SUBLANES, CV_WIDTH), F32),
        ],
        compiler_params=_cparams(("arbitrary", "arbitrary")),
        name=name,
    )(z3, z3, z3, z3, z3, z3, dw3, db3, lg3, lb3, pw3, pb3, gg3)


def kernel(x, mem, g_mix, w_in, g_q, g_k, rg_conv_w, rg_conv_b, rg_w_a, rg_b_a, rg_w_x, rg_b_x, rg_lam, cv_dw_w, cv_dw_b, cv_ln_g, cv_ln_b, cv_pw_w, cv_pw_b, g_grp, w_out, g_xattn, g_mem, xa_wq, xa_wk, xa_wv, xa_wo, g_ffn, ffn_wg, ffn_wu, ffn_wd, g_final):
    b, seq, d = x.shape
    n_mem = mem.shape[1]
    depth = w_in.shape[0]
    m = b * seq
    row3 = lambda p: p.reshape(p.shape[0], 1, p.shape[-1])

    cos_t, sin_t = rope_tables(seq)
    g_q3, g_k3, g_grp3, g_mem3 = row3(g_q), row3(g_k), row3(g_grp), row3(g_mem)
    g_final3 = g_final.reshape(1, 1, d)
    rg_cb3, cv_db3, cv_lg3, cv_lb3, cv_pb3 = (row3(rg_conv_b), row3(cv_dw_b), row3(cv_ln_g),
                                              row3(cv_ln_b), row3(cv_pw_b))
    rg_w4 = jnp.concatenate([rg_w_a[:, 0], rg_w_x[:, 0], rg_w_a[:, 1], rg_w_x[:, 1]], axis=-1)
    rg_w4 = rg_w4.reshape(-1, RG_BLOCK, 4 * RG_BLOCK)

    col3 = lambda p: jnp.broadcast_to(p[:, :, None], (p.shape[0], p.shape[1], V7X_LANES))
    g_mix_c, g_xattn_c, g_ffn_c = col3(g_mix), col3(g_xattn), col3(g_ffn)
    wd_bf16 = cast_bf16(ffn_wd, "cast_wd")

    xf = x.reshape(m, d)
    memf = mem.reshape(b * n_mem, d)
    xb, xr = row_stats(xf, "x_stats")

    for l in range(depth):
        last = l == depth - 1
        tm, tn = TILE_IN_PROJ
        z = matmul([xb], w_in, l, F32, f"in_proj{l}", tn=tn, tm=tm, norm_in=(g_mix_c, xr),
                   w_chunks=W_PREFETCH_CHUNKS)
        z3 = z.reshape(b, seq, z.shape[1])
        y_attn = attention(z, g_q3, g_k3, cos_t, sin_t, l, b, seq, f"attn{l}")
        y_rec = rglru(z3, rg_conv_w, rg_cb3, rg_w4, rg_b_a, rg_b_x, rg_lam, l, f"rglru{l}")
        y_cv_n = conformer(z3, cv_dw_w, cv_db3, cv_lg3, cv_lb3, cv_pw_w, cv_pb3, g_grp3, l, f"conformer{l}")
        y_attn_n = rmsnorm(y_attn, g_grp3, l, 0, BF16, f"norm_attn{l}")
        y_rec_n = rmsnorm(y_rec.reshape(m, RG_WIDTH), g_grp3, l, ATTN_WIDTH // RG_WIDTH, BF16, f"norm_rec{l}")
        tm, tn = TILE_OUT_PROJ
        xf, xb, xr = matmul([y_attn_n, y_rec_n, y_cv_n.reshape(m, CV_WIDTH)], w_out, l, F32, f"out_proj{l}",
                            tn=tn, tm=tm, res=xf, emit_norm=True, single_buffer_w=True)

        mn = rmsnorm(memf, g_mem3, l, 0, BF16, f"norm_mem{l}")
        tm, tn = TILE_XA_KV
        k2 = matmul([mn], xa_wk, l, BF16, f"xa_k{l}", tn=tn, tm=tm)
        v2 = matmul([mn], xa_wv, l, BF16, f"xa_v{l}", tn=tn, tm=tm)
        o2 = q_proj_cross_attention(xb, xa_wq, g_xattn_c, xr, k2, v2, l, seq, n_mem, f"xa_q_attn{l}",
                                    tm=TILE_XA_Q, w_chunks=W_PREFETCH_CHUNKS)
        tm, tn = TILE_XA_O
        xf, xb, xr = matmul([o2], xa_wo, l, F32, f"xa_o{l}", tn=tn, tm=tm, res=xf, emit_norm=True,
                            single_buffer_w=True)

        tm, tn = TILE_FFN_UP
        act = swiglu(xb, ffn_wg, ffn_wu, g_ffn_c, xr, l, f"ffn_up{l}", tn=tn, tm=tm,
                     w_chunks=W_PREFETCH_CHUNKS)
        tm, tn = TILE_FFN_DOWN
        if last:
            xf = matmul([act], wd_bf16, l, F32, f"ffn_down{l}", tn=tn, tm=tm, res=xf)
        else:
            xf, xb, xr = matmul([act], wd_bf16, l, F32, f"ffn_down{l}", tn=tn, tm=tm, res=xf, emit_norm=True)

    out = rmsnorm(xf, g_final3, 0, 0, F32, "norm_final")
    return out.reshape(b, seq, d)
```

```python
import functools
import math

import jax
import jax.numpy as jnp
from jax import lax
from jax.experimental import pallas as pl
from jax.experimental.pallas import tpu as pltpu

F32 = jnp.float32
BF16 = jnp.bfloat16

HEAD_DIM = 128
N_Q_HEADS = 16
N_KV_HEADS = 4
Q_PER_KV = N_Q_HEADS // N_KV_HEADS
ATTN_WIDTH = N_Q_HEADS * HEAD_DIM
KV_WIDTH = N_KV_HEADS * HEAD_DIM
RG_WIDTH = 1024
RG_BLOCK = 128
RG_CONV_W = 4
RG_C = 8.0
CV_WIDTH = 1024
CV_KERNEL = 31
QKV_WIDTH = ATTN_WIDTH + 2 * KV_WIDTH
COL_RX = QKV_WIDTH // RG_WIDTH
COL_RG = COL_RX + 1
COL_CA = COL_RG + 1
COL_CB = COL_CA + 1
MEM_HEADS = 4
GRID_W = 64
ROPE_THETA = 10000.0
ROPE_AXIS_DIM = HEAD_DIM // 2
EPS = 1e-6

V7X_VMEM_BYTES = 64 * 1024 * 1024
V7X_SUBLANES = 8
V7X_LANES = 128
VMEM_LIMIT = V7X_VMEM_BYTES - 4 * 1024 * 1024

CV_HALO = 16


def _sigmoid(x):
    return 0.5 * jnp.tanh(0.5 * x) + 0.5


def _cparams(sem):
    return pltpu.CompilerParams(dimension_semantics=sem, vmem_limit_bytes=VMEM_LIMIT)


def _rmsnorm_kernel(x_ref, g_ref, o_ref):
    x = x_ref[...].astype(F32)
    ms = jnp.mean(x * x, axis=-1, keepdims=True)
    o_ref[...] = (x * lax.rsqrt(ms + EPS) * g_ref[...]).astype(o_ref.dtype)


def rmsnorm(x, g3, layer, gcol, out_dtype, name, tm=512):
    m, w = x.shape
    tm = min(tm, m)
    return pl.pallas_call(
        _rmsnorm_kernel,
        out_shape=jax.ShapeDtypeStruct((m, w), out_dtype),
        grid=(m // tm,),
        in_specs=[
            pl.BlockSpec((tm, w), lambda i: (i, 0)),
            pl.BlockSpec((None, 1, w), lambda i: (layer, 0, gcol)),
        ],
        out_specs=pl.BlockSpec((tm, w), lambda i: (i, 0)),
        compiler_params=_cparams(("arbitrary",)),
        name=name,
    )(x, g3)


CAST_ROWS = 256
W_PREFETCH_CHUNKS = 4

TILE_IN_PROJ = (512, 1024)
TILE_OUT_PROJ = (512, 1024)
TILE_XA_KV = (1024, 512)
TILE_XA_Q = 512
TILE_XA_O = (512, 1024)
TILE_FFN_UP = (1024, 256)
TILE_FFN_DOWN = (512, 512)


def _cast_panel(w_refs, wb_ref, g_ref=None):
    kc, tn = w_refs[0].shape

    for ci, w_ref in enumerate(w_refs):
        def body(i, c, w_ref=w_ref, base=ci * kc):
            src = pl.ds(pl.multiple_of(i * CAST_ROWS, CAST_ROWS), CAST_ROWS)
            dst = pl.ds(pl.multiple_of(base + i * CAST_ROWS, CAST_ROWS), CAST_ROWS)
            if g_ref is None:
                wb_ref[dst, :] = w_ref[src, :].astype(BF16)
            else:
                g = g_ref[dst, :]
                for c0 in range(0, tn, V7X_LANES):
                    wb_ref[dst, c0:c0 + V7X_LANES] = (w_ref[src, c0:c0 + V7X_LANES] * g).astype(BF16)
            return c

        lax.fori_loop(0, kc // CAST_ROWS, body, 0)


def _n_w_chunks(requested, n_row_tiles):
    limit = max(1, min(requested, n_row_tiles - 1))
    return 1 << (limit.bit_length() - 1)


def _w_chunk_specs(k, tn, layer, n_panels, n_chunks):
    kc = k // n_chunks
    assert kc * n_chunks == k and kc % CAST_ROWS == 0

    def spec(c):
        def index(j, i):
            return layer, c, jnp.where(i <= c, j, jnp.minimum(j + 1, n_panels - 1))
        return pl.BlockSpec((None, kc, tn), index)

    return [spec(c) for c in range(n_chunks)]


STAT_LANES = V7X_LANES


def _mm_kernel(*refs, k_splits, n_w, has_norm_in, has_res, emit_norm, w_is_bf16, n_total):
    n_a = len(k_splits)
    a_refs = refs[:n_a]
    w_refs = refs[n_a:n_a + n_w]
    pos = n_a + n_w
    g_ref = rin_ref = res_ref = xb_ref = rout_ref = ssq_ref = None
    if has_norm_in:
        g_ref, rin_ref = refs[pos], refs[pos + 1]
        pos += 2
    if has_res:
        res_ref = refs[pos]
        pos += 1
    o_ref = refs[pos]
    pos += 1
    if emit_norm:
        xb_ref, rout_ref = refs[pos], refs[pos + 1]
        pos += 2
    j = pl.program_id(0)
    i = pl.program_id(1)
    if w_is_bf16:
        wb_ref = w_refs[0]
    else:
        wb_ref = refs[pos]
        pos += 1

        @pl.when(i == 0)
        def _():
            _cast_panel(w_refs, wb_ref, g_ref)

    acc = None
    off = 0
    for a_ref, kk in zip(a_refs, k_splits):
        d = jnp.dot(a_ref[...], wb_ref[off:off + kk, :], preferred_element_type=F32)
        acc = d if acc is None else acc + d
        off += kk
    if has_norm_in:
        acc = acc * rin_ref[:, 0:1]
    if has_res:
        acc = acc + res_ref[...]
    o_ref[...] = acc.astype(o_ref.dtype)
    if emit_norm:
        ssq_ref = refs[pos]
        tm = acc.shape[0]
        xb_ref[...] = acc.astype(BF16)
        rows = pl.ds(pl.multiple_of(i * tm, tm), tm)
        part = jnp.broadcast_to(jnp.sum(acc * acc, axis=-1, keepdims=True), (tm, STAT_LANES))

        @pl.when(j == 0)
        def _():
            ssq_ref[rows, :] = part

        @pl.when(j > 0)
        def _():
            ssq_ref[rows, :] = ssq_ref[rows, :] + part

        rout_ref[...] = lax.rsqrt(ssq_ref[rows, :] * (1.0 / n_total) + EPS)


def matmul(a_list, w3, layer, out_dtype, name, *, tn, tm, norm_in=None, res=None, emit_norm=False,
           single_buffer_w=False, w_chunks=1):
    m = a_list[0].shape[0]
    k_splits = tuple(a.shape[1] for a in a_list)
    _, k, n = w3.shape
    assert sum(k_splits) == k
    tm = min(tm, m)
    w_is_bf16 = w3.dtype == BF16
    assert not (w_is_bf16 and norm_in is not None)
    in_specs = [pl.BlockSpec((tm, kk), lambda j, i: (i, 0)) for kk in k_splits]
    assert w_chunks == 1 or not (w_is_bf16 or single_buffer_w)
    w_chunks = _n_w_chunks(w_chunks, m // tm)
    if w_chunks > 1:
        in_specs += _w_chunk_specs(k, tn, layer, n // tn, w_chunks)
    else:
        w_mode = dict(pipeline_mode=pl.Buffered(1)) if single_buffer_w else {}
        in_specs.append(pl.BlockSpec((None, k, tn), lambda j, i: (layer, 0, j), **w_mode))
    args = list(a_list) + [w3] * w_chunks
    if norm_in is not None:
        last_slab = norm_in[1].shape[0] - 1
        in_specs.append(pl.BlockSpec((None, k, V7X_LANES), lambda j, i: (layer, 0, 0)))
        in_specs.append(pl.BlockSpec((None, tm, STAT_LANES), lambda j, i: (last_slab, i, 0)))
        args += list(norm_in)
    if res is not None:
        in_specs.append(pl.BlockSpec((tm, tn), lambda j, i: (i, j)))
        args.append(res)
    out_shape = [jax.ShapeDtypeStruct((m, n), out_dtype)]
    out_specs = [pl.BlockSpec((tm, tn), lambda j, i: (i, j))]
    scratch = [] if w_is_bf16 else [pltpu.VMEM((k, tn), BF16)]
    if emit_norm:
        out_shape += [jax.ShapeDtypeStruct((m, n), BF16), jax.ShapeDtypeStruct((n // tn, m, STAT_LANES), F32)]
        out_specs += [pl.BlockSpec((tm, tn), lambda j, i: (i, j)),
                      pl.BlockSpec((None, tm, STAT_LANES), lambda j, i: (j, i, 0))]
        scratch.append(pltpu.VMEM((m, STAT_LANES), F32))
    kern = functools.partial(_mm_kernel, k_splits=k_splits, n_w=w_chunks, has_norm_in=norm_in is not None,
                             has_res=res is not None, emit_norm=emit_norm, w_is_bf16=w_is_bf16, n_total=n)
    out = pl.pallas_call(
        kern,
        out_shape=tuple(out_shape),
        grid=(n // tn, m // tm),
        in_specs=in_specs,
        out_specs=tuple(out_specs),
        scratch_shapes=scratch,
        compiler_params=_cparams(("arbitrary", "arbitrary")),
        name=name,
    )(*args)
    return out if emit_norm else out[0]


def _row_stats_kernel(x_ref, xb_ref, r_ref):
    x = x_ref[...]
    xb_ref[...] = x.astype(BF16)
    ms = jnp.mean(x * x, axis=-1, keepdims=True)
    r_ref[...] = jnp.broadcast_to(lax.rsqrt(ms + EPS), r_ref.shape)


def row_stats(x, name, tm=512):
    m, w = x.shape
    tm = min(tm, m)
    return pl.pallas_call(
        _row_stats_kernel,
        out_shape=(jax.ShapeDtypeStruct((m, w), BF16), jax.ShapeDtypeStruct((1, m, STAT_LANES), F32)),
        grid=(m // tm,),
        in_specs=[pl.BlockSpec((tm, w), lambda i: (i, 0))],
        out_specs=(pl.BlockSpec((tm, w), lambda i: (i, 0)),
                   pl.BlockSpec((None, tm, STAT_LANES), lambda i: (0, i, 0))),
        compiler_params=_cparams(("arbitrary",)),
        name=name,
    )(x)


def _swiglu_kernel(*refs, n_w):
    a_ref = refs[0]
    wg_refs = refs[1:1 + n_w]
    wu_refs = refs[1 + n_w:1 + 2 * n_w]
    g_ref, rin_ref, o_ref, wgb_ref, wub_ref = refs[1 + 2 * n_w:]

    @pl.when(pl.program_id(1) == 0)
    def _():
        _cast_panel(wg_refs, wgb_ref, g_ref)
        _cast_panel(wu_refs, wub_ref, g_ref)

    a = a_ref[...]
    r = rin_ref[:, 0:1]
    g = jnp.dot(a, wgb_ref[...], preferred_element_type=F32) * r
    u = jnp.dot(a, wub_ref[...], preferred_element_type=F32) * r
    o_ref[...] = (g * _sigmoid(g) * u).astype(o_ref.dtype)


def swiglu(a, wg3, wu3, g3, rin, layer, name, *, tn, tm, w_chunks):
    m, k = a.shape
    n = wg3.shape[2]
    tm = min(tm, m)
    assert n % tn == 0
    w_chunks = _n_w_chunks(w_chunks, m // tm)
    wspecs = _w_chunk_specs(k, tn, layer, n // tn, w_chunks)
    last_slab = rin.shape[0] - 1
    return pl.pallas_call(
        functools.partial(_swiglu_kernel, n_w=w_chunks),
        out_shape=jax.ShapeDtypeStruct((m, n), BF16),
        grid=(n // tn, m // tm),
        in_specs=[pl.BlockSpec((tm, k), lambda j, i: (i, 0))] + wspecs + wspecs + [
            pl.BlockSpec((None, k, V7X_LANES), lambda j, i: (layer, 0, 0)),
            pl.BlockSpec((None, tm, STAT_LANES), lambda j, i: (last_slab, i, 0))],
        out_specs=pl.BlockSpec((tm, tn), lambda j, i: (i, j)),
        scratch_shapes=[pltpu.VMEM((k, tn), BF16), pltpu.VMEM((k, tn), BF16)],
        compiler_params=_cparams(("arbitrary", "arbitrary")),
        name=name,
    )(a, *([wg3] * w_chunks), *([wu3] * w_chunks), g3, rin)


def _cast_kernel(x_ref, o_ref):
    o_ref[...] = x_ref[...].astype(o_ref.dtype)


def cast_bf16(w3, name, rows=256):
    l, k, n = w3.shape
    return pl.pallas_call(
        _cast_kernel,
        out_shape=jax.ShapeDtypeStruct((l, k, n), BF16),
        grid=(l, k // rows),
        in_specs=[pl.BlockSpec((None, rows, n), lambda a, i: (a, i, 0))],
        out_specs=pl.BlockSpec((None, rows, n), lambda a, i: (a, i, 0)),
        compiler_params=_cparams(("arbitrary", "arbitrary")),
        name=name,
    )(w3)


def rope_tables(seq):
    rows = seq // GRID_W
    row = jnp.repeat(jnp.arange(rows), GRID_W).astype(F32)
    col = jnp.tile(jnp.arange(GRID_W), rows).astype(F32)
    inv = ROPE_THETA ** (-jnp.arange(0, ROPE_AXIS_DIM, 2, dtype=F32) / ROPE_AXIS_DIM)
    ang_r = row[:, None] * inv
    ang_c = col[:, None] * inv
    cos_t = jnp.concatenate([jnp.cos(ang_r), jnp.cos(ang_r), jnp.cos(ang_c), jnp.cos(ang_c)], axis=1)
    sin_t = jnp.concatenate([-jnp.sin(ang_r), jnp.sin(ang_r), -jnp.sin(ang_c), jnp.sin(ang_c)], axis=1)
    return cos_t, sin_t


ATTN_KEY_CHUNK = 512


def _norm_rope(x, g, cos, sin):
    half = ROPE_AXIS_DIM // 2
    lane = lax.broadcasted_iota(jnp.int32, (1, HEAD_DIM), 1)
    first_half = (lane % ROPE_AXIS_DIM) < half
    ms = jnp.mean(x * x, axis=-1, keepdims=True)
    xn = x * lax.rsqrt(ms + EPS) * g
    swapped = jnp.where(first_half, pltpu.roll(xn, HEAD_DIM - half, axis=1), pltpu.roll(xn, half, axis=1))
    return xn * cos + swapped * sin


def _attn_kernel(zq_ref, zk_ref, zv_ref, gq_ref, gk_ref, cos_ref, sin_ref, o_ref, k_s, v_s, *, tq, seq):
    i = pl.program_id(2)

    @pl.when(i == 0)
    def _():
        k_s[...] = _norm_rope(zk_ref[...], gk_ref[...], cos_ref[...], sin_ref[...]).astype(BF16)
        v_s[:, 0:HEAD_DIM] = zv_ref[...].astype(BF16)
        v_s[:, HEAD_DIM:2 * HEAD_DIM] = jnp.ones((seq, HEAD_DIM), BF16)

    r0 = pl.multiple_of(i * tq, tq)
    cos = cos_ref[pl.ds(r0, tq), :]
    sin = sin_ref[pl.ds(r0, tq), :]
    gq = gq_ref[...]
    k = k_s[...]
    v = v_s[...]
    c = (HEAD_DIM ** -0.5) * math.log2(math.e)
    for h in range(Q_PER_KV):
        c0 = h * HEAD_DIM
        q = _norm_rope(zq_ref[:, c0:c0 + HEAD_DIM], gq, cos, sin).astype(BF16)
        mx = o = None
        for k0 in range(0, seq, ATTN_KEY_CHUNK):
            kc = k[k0:k0 + ATTN_KEY_CHUNK, :]
            s = lax.dot_general(q, kc, (((1,), (1,)), ((), ())), preferred_element_type=F32)
            mc = jnp.max(s, axis=-1, keepdims=True)
            if mx is None:
                mx = mc
                p = jnp.exp2((s - mx) * c).astype(BF16)
                o = jnp.dot(p, v[k0:k0 + ATTN_KEY_CHUNK, :], preferred_element_type=F32)
            else:
                mn = jnp.maximum(mx, mc)
                alpha = jnp.exp2((mx - mn) * c)
                p = jnp.exp2((s - mn) * c).astype(BF16)
                o = o * alpha + jnp.dot(p, v[k0:k0 + ATTN_KEY_CHUNK, :], preferred_element_type=F32)
                mx = mn
        o_ref[:, c0:c0 + HEAD_DIM] = (o[:, 0:HEAD_DIM] / o[:, HEAD_DIM:2 * HEAD_DIM]).astype(o_ref.dtype)


def attention(z, gq3, gk3, cos_t, sin_t, layer, batch, seq, name, tq=256):
    m = z.shape[0]
    tq = min(tq, seq)
    nq = seq // tq
    gw = Q_PER_KV * HEAD_DIM
    kcol = ATTN_WIDTH // HEAD_DIM
    vcol = (ATTN_WIDTH + KV_WIDTH) // HEAD_DIM
    gspec = pl.BlockSpec((None, 1, HEAD_DIM), lambda b, g, i: (layer, 0, 0))
    tspec = pl.BlockSpec((seq, HEAD_DIM), lambda b, g, i: (0, 0))
    kern = functools.partial(_attn_kernel, tq=tq, seq=seq)
    return pl.pallas_call(
        kern,
        out_shape=jax.ShapeDtypeStruct((m, ATTN_WIDTH), F32),
        grid=(batch, N_KV_HEADS, nq),
        in_specs=[
            pl.BlockSpec((tq, gw), lambda b, g, i: (b * nq + i, g)),
            pl.BlockSpec((seq, HEAD_DIM), lambda b, g, i: (b, kcol + g)),
            pl.BlockSpec((seq, HEAD_DIM), lambda b, g, i: (b, vcol + g)),
            gspec, gspec, tspec, tspec,
        ],
        out_specs=pl.BlockSpec((tq, gw), lambda b, g, i: (b * nq + i, g)),
        scratch_shapes=[pltpu.VMEM((seq, HEAD_DIM), BF16), pltpu.VMEM((seq, 2 * HEAD_DIM), BF16)],
        compiler_params=_cparams(("arbitrary", "arbitrary", "arbitrary")),
        name=name,
    )(z, z, z, gq3, gk3, cos_t, sin_t)


def _xq_attn_kernel(*refs, n_w):
    x_ref = refs[0]
    w_refs = refs[1:1 + n_w]
    g_ref, rin_ref, k_ref, v_ref, o_ref, wb_ref = refs[1 + n_w:]

    @pl.when(pl.program_id(1) == 0)
    def _():
        _cast_panel(w_refs, wb_ref, g_ref)

    head_dim = wb_ref.shape[1]
    q = jnp.dot(x_ref[...], wb_ref[...], preferred_element_type=F32) * rin_ref[:, 0:1]
    k = k_ref[...]
    v = v_ref[...]
    s = lax.dot_general(q.astype(BF16), k, (((1,), (1,)), ((), ())), preferred_element_type=F32)
    s = s * (head_dim ** -0.5)
    mx = jnp.max(s, axis=-1, keepdims=True)
    p = jnp.exp(s - mx)
    den = jnp.sum(p, axis=-1, keepdims=True)
    o = jnp.dot(p.astype(BF16), v, preferred_element_type=F32)
    o_ref[...] = (o / den).astype(o_ref.dtype)


def q_proj_cross_attention(xb, wq3, g3, rin, k, v, layer, seq, n_mem, name, tm=512, w_chunks=1):
    m, d = xb.shape
    head_dim = d // MEM_HEADS
    tm = min(tm, seq)
    tiles_per_batch = seq // tm
    last_slab = rin.shape[0] - 1
    w_chunks = _n_w_chunks(w_chunks, m // tm)
    kvspec = pl.BlockSpec((n_mem, head_dim), lambda j, i: (i // tiles_per_batch, j))
    return pl.pallas_call(
        functools.partial(_xq_attn_kernel, n_w=w_chunks),
        out_shape=jax.ShapeDtypeStruct((m, d), BF16),
        grid=(MEM_HEADS, m // tm),
        in_specs=[pl.BlockSpec((tm, d), lambda j, i: (i, 0))]
        + _w_chunk_specs(d, head_dim, layer, MEM_HEADS, w_chunks) + [
            pl.BlockSpec((None, d, V7X_LANES), lambda j, i: (layer, 0, 0)),
            pl.BlockSpec((None, tm, STAT_LANES), lambda j, i: (last_slab, i, 0)),
            kvspec, kvspec,
        ],
        out_specs=pl.BlockSpec((tm, head_dim), lambda j, i: (i, j)),
        scratch_shapes=[pltpu.VMEM((d, head_dim), BF16)],
        compiler_params=_cparams(("arbitrary", "arbitrary")),
        name=name,
    )(xb, *([wq3] * w_chunks), g3, rin, k, v)


RG_PAD = V7X_SUBLANES


def _rglru_kernel(x_ref, gate_ref, cw_ref, cb_ref, w_ref, ba_ref, bx_ref, lam_ref, y_ref,
                  pad_ref, af_ref, uf_ref, ab_ref, ub_ref, *, seq, heads):
    cw = heads * RG_BLOCK
    zeros = jnp.zeros((RG_PAD, cw), F32)
    pad_ref[0:RG_PAD, :] = zeros
    pad_ref[RG_PAD + seq:RG_PAD + seq + RG_PAD, :] = zeros
    pad_ref[RG_PAD:RG_PAD + seq, :] = x_ref[...]
    left = RG_CONV_W // 2
    xc = cb_ref[...] + cw_ref[0:1, :] * pad_ref[pl.ds(RG_PAD - left, seq), :]
    for j in range(1, RG_CONV_W):
        xc = xc + cw_ref[j:j + 1, :] * pad_ref[pl.ds(RG_PAD - left + j, seq), :]
    xcb = xc.astype(BF16)

    lam = lam_ref[...]
    nl = -lam
    softplus = jnp.maximum(nl, 0.0) + jnp.log1p(jnp.exp(-jnp.abs(nl)))
    ba = ba_ref[...]
    bx = bx_ref[...]
    for h in range(heads):
        c0 = h * RG_BLOCK
        sl = slice(c0, c0 + RG_BLOCK)
        gates = jnp.dot(xcb[:, sl], w_ref[h].astype(BF16), preferred_element_type=F32)
        xch = xc[:, sl]
        for d, (a_ref, u_ref) in enumerate(((af_ref, uf_ref), (ab_ref, ub_ref))):
            g0 = 2 * d * RG_BLOCK
            r = _sigmoid(gates[:, g0:g0 + RG_BLOCK] + ba[d:d + 1, sl])
            i = _sigmoid(gates[:, g0 + RG_BLOCK:g0 + 2 * RG_BLOCK] + bx[d:d + 1, sl])
            log_a = (-RG_C) * r * softplus[d:d + 1, sl]
            t = jnp.tanh(log_a)
            one_minus_a2 = (-2.0 * t) / (1.0 - t)
            a_ref[:, sl] = jnp.exp(log_a)
            u_ref[:, sl] = jnp.sqrt(one_minus_a2) * (i * xch)

    n_tiles = seq // V7X_SUBLANES
    row = lax.broadcasted_iota(jnp.int32, (V7X_SUBLANES, cw), 0)

    def tile_scan(a, u, reverse):
        for d in (1, 2, 4):
            if reverse:
                shift, keep = V7X_SUBLANES - d, row < V7X_SUBLANES - d
            else:
                shift, keep = d, row >= d
            a_s = pltpu.roll(a, shift, axis=0)
            u_s = pltpu.roll(u, shift, axis=0)
            u = jnp.where(keep, u + a * u_s, u)
            a = jnp.where(keep, a * a_s, a)
        return a, u

    def body(t, carry):
        hf, hb = carry
        rf = pl.multiple_of(t * V7X_SUBLANES, V7X_SUBLANES)
        a, u = tile_scan(af_ref[pl.ds(rf, V7X_SUBLANES), :], uf_ref[pl.ds(rf, V7X_SUBLANES), :], False)
        hh = a * hf + u
        uf_ref[pl.ds(rf, V7X_SUBLANES), :] = hh
        hf = jnp.broadcast_to(hh[V7X_SUBLANES - 1:V7X_SUBLANES, :], hh.shape)
        rb = pl.multiple_of((n_tiles - 1 - t) * V7X_SUBLANES, V7X_SUBLANES)
        a, u = tile_scan(ab_ref[pl.ds(rb, V7X_SUBLANES), :], ub_ref[pl.ds(rb, V7X_SUBLANES), :], True)
        hh = a * hb + u
        ub_ref[pl.ds(rb, V7X_SUBLANES), :] = hh
        hb = jnp.broadcast_to(hh[0:1, :], hh.shape)
        return hf, hb

    h0 = jnp.zeros((V7X_SUBLANES, cw), F32)
    lax.fori_loop(0, n_tiles, body, (h0, h0))

    g = gate_ref[...]
    gelu = 0.5 * g * (1.0 + jnp.tanh(math.sqrt(2.0 / math.pi) * (g + 0.044715 * (g * g * g))))
    y_ref[...] = ((uf_ref[...] + ub_ref[...]) * gelu).astype(y_ref.dtype)


def rglru(z3, cw3, cb3, w4, ba3, bx3, lam3, layer, name, heads=2):
    b, seq, _ = z3.shape
    cw = heads * RG_BLOCK
    ncb = RG_WIDTH // cw
    n_rg_heads = RG_WIDTH // RG_BLOCK
    kern = functools.partial(_rglru_kernel, seq=seq, heads=heads)
    vec = lambda rows: pl.BlockSpec((None, rows, cw), lambda bi, j: (layer, 0, j))
    return pl.pallas_call(
        kern,
        out_shape=jax.ShapeDtypeStruct((b, seq, RG_WIDTH), F32),
        grid=(b, ncb),
        in_specs=[
            pl.BlockSpec((None, seq, cw), lambda bi, j: (bi, 0, COL_RX * ncb + j)),
            pl.BlockSpec((None, seq, cw), lambda bi, j: (bi, 0, COL_RG * ncb + j)),
            vec(RG_CONV_W), vec(1),
            pl.BlockSpec((heads, RG_BLOCK, 4 * RG_BLOCK), lambda bi, j: ((layer * n_rg_heads) // heads + j, 0, 0)),
            vec(2), vec(2), vec(2),
        ],
        out_specs=pl.BlockSpec((None, seq, cw), lambda bi, j: (bi, 0, j)),
        scratch_shapes=[pltpu.VMEM((seq + 2 * RG_PAD, cw), F32)] + [pltpu.VMEM((seq, cw), F32)] * 4,
        compiler_params=_cparams(("arbitrary", "arbitrary")),
        name=name,
    )(z3, z3, cw3, cb3, w4, ba3, bx3, lam3)


CV_CHUNK = 32


def _conformer_kernel(a_ref, g_ref, ap_ref, gp_ref, an_ref, gn_ref, dw_ref, db_ref, lg_ref, lb_ref,
                      pw_ref, pb_ref, gg_ref, o_ref, buf_ref, sh_ref, conv_ref, pwb_ref, dwb_ref, *, ts):
    i = pl.program_id(1)
    n_i = pl.num_programs(1)

    @pl.when(jnp.logical_and(pl.program_id(0) == 0, i == 0))
    def _():
        pwb_ref[...] = pw_ref[...].astype(BF16)
        for j in range(CV_KERNEL):
            dwb_ref[j] = jnp.broadcast_to(dw_ref[j:j + 1, :], (V7X_SUBLANES, CV_WIDTH))

    def glu(a, g):
        return a * _sigmoid(g)

    prev = glu(ap_ref[...], gp_ref[...])
    nxt = glu(an_ref[...], gn_ref[...])
    buf_ref[0:CV_HALO, :] = jnp.where(i > 0, prev, 0.0)
    buf_ref[CV_HALO:CV_HALO + ts, :] = glu(a_ref[...], g_ref[...])
    buf_ref[CV_HALO + ts:CV_HALO + ts + CV_HALO, :] = jnp.where(i < n_i - 1, nxt, 0.0)

    span = ts + 2 * CV_HALO - V7X_SUBLANES
    for r in range(V7X_SUBLANES):
        sh_ref[r] = buf_ref[pl.ds(r, span), :]

    base_off = CV_HALO - CV_KERNEL // 2
    n_sub = CV_CHUNK // V7X_SUBLANES
    bias = jnp.broadcast_to(db_ref[...], (V7X_SUBLANES, CV_WIDTH))

    def chunk(c, carry):
        r0 = pl.multiple_of(c * CV_CHUNK, CV_CHUNK)
        accs = [bias] * n_sub
        for j in range(CV_KERNEL):
            off = base_off + j
            q, r = off // V7X_SUBLANES, off % V7X_SUBLANES
            w = dwb_ref[j]
            for s in range(n_sub):
                rows = pl.multiple_of(r0 + (q + s) * V7X_SUBLANES, V7X_SUBLANES)
                accs[s] = accs[s] + w * sh_ref[r, pl.ds(rows, V7X_SUBLANES), :]
        for s in range(n_sub):
            rows = pl.multiple_of(r0 + s * V7X_SUBLANES, V7X_SUBLANES)
            conv_ref[pl.ds(rows, V7X_SUBLANES), :] = accs[s]
        return carry

    lax.fori_loop(0, ts // CV_CHUNK, chunk, 0)

    y = conv_ref[...]
    mu = jnp.mean(y, axis=-1, keepdims=True)
    yc = y - mu
    var = jnp.mean(yc * yc, axis=-1, keepdims=True)
    yn = yc * lax.rsqrt(var + EPS) * lg_ref[...] + lb_ref[...]
    act = yn * _sigmoid(yn)
    out = jnp.dot(act.astype(BF16), pwb_ref[...], preferred_element_type=F32) + pb_ref[...]
    ms = jnp.mean(out * out, axis=-1, keepdims=True)
    o_ref[...] = (out * lax.rsqrt(ms + EPS) * gg_ref[...]).astype(o_ref.dtype)


def conformer(z3, dw3, db3, lg3, lb3, pw3, pb3, gg3, layer, name, ts=256):
    b, seq, _ = z3.shape
    ts = min(ts, seq)
    nt = seq // ts
    hb = ts // CV_HALO
    n_hblk = seq // CV_HALO
    main = lambda col: pl.BlockSpec((None, ts, CV_WIDTH), lambda bi, i: (bi, i, col))
    prev = lambda col: pl.BlockSpec((None, CV_HALO, CV_WIDTH),
                                    lambda bi, i: (bi, jnp.maximum(i * hb - 1, 0), col))
    nxt = lambda col: pl.BlockSpec((None, CV_HALO, CV_WIDTH),
                                   lambda bi, i: (bi, jnp.minimum((i + 1) * hb, n_hblk - 1), col))
    vec = lambda rows: pl.BlockSpec((None, rows, CV_WIDTH), lambda bi, i: (layer, 0, 0))
    kern = functools.partial(_conformer_kernel, ts=ts)
    return pl.pallas_call(
        kern,
        out_shape=jax.ShapeDtypeStruct((b, seq, CV_WIDTH), BF16),
        grid=(b, nt),
        in_specs=[
            main(COL_CA), main(COL_CB), prev(COL_CA), prev(COL_CB), nxt(COL_CA), nxt(COL_CB),
            vec(CV_KERNEL), vec(1), vec(1), vec(1),
            pl.BlockSpec((None, CV_WIDTH, CV_WIDTH), lambda bi, i: (layer, 0, 0)),
            vec(1),
            pl.BlockSpec((None, 1, CV_WIDTH), lambda bi, i: (layer, 0, (ATTN_WIDTH + RG_WIDTH) // CV_WIDTH)),
        ],
        out_specs=pl.BlockSpec((None, ts, CV_WIDTH), lambda bi, i: (bi, i, 0)),
        scratch_shapes=[
            pltpu.VMEM((ts + 2 * CV_HALO, CV_WIDTH), F32),
            pltpu.VMEM((V7X_SUBLANES, ts + 2 * CV_HALO - V7X_SUBLANES, CV_WIDTH), F32),
            pltpu.VMEM((ts, CV_WIDTH), F32),
            pltpu.VMEM((CV_WIDTH, CV_WIDTH), BF16),
            pltpu.VMEM((CV_KERNEL, V7X_SUBLANES, CV_WIDTH), F32),
        ],
        compiler_params=_cparams(("arbitrary", "arbitrary")),
        name=name,
    )(z3, z3, z3, z3, z3, z3, dw3, db3, lg3, lb3, pw3, pb3, gg3)


def kernel(x, mem, g_mix, w_in, g_q, g_k, rg_conv_w, rg_conv_b, rg_w_a, rg_b_a, rg_w_x, rg_b_x, rg_lam, cv_dw_w, cv_dw_b, cv_ln_g, cv_ln_b, cv_pw_w, cv_pw_b, g_grp, w_out, g_xattn, g_mem, xa_wq, xa_wk, xa_wv, xa_wo, g_ffn, ffn_wg, ffn_wu, ffn_wd, g_final):
    b, seq, d = x.shape
    n_mem = mem.shape[1]
    depth = w_in.shape[0]
    m = b * seq
    row3 = lambda p: p.reshape(p.shape[0], 1, p.shape[-1])

    cos_t, sin_t = rope_tables(seq)
    g_q3, g_k3, g_grp3, g_mem3 = row3(g_q), row3(g_k), row3(g_grp), row3(g_mem)
    g_final3 = g_final.reshape(1, 1, d)
    rg_cb3, cv_db3, cv_lg3, cv_lb3, cv_pb3 = (row3(rg_conv_b), row3(cv_dw_b), row3(cv_ln_g),
                                              row3(cv_ln_b), row3(cv_pw_b))
    rg_w4 = jnp.concatenate([rg_w_a[:, 0], rg_w_x[:, 0], rg_w_a[:, 1], rg_w_x[:, 1]], axis=-1)
    rg_w4 = rg_w4.reshape(-1, RG_BLOCK, 4 * RG_BLOCK)

    col3 = lambda p: jnp.broadcast_to(p[:, :, None], (p.shape[0], p.shape[1], V7X_LANES))
    g_mix_c, g_xattn_c, g_ffn_c = col3(g_mix), col3(g_xattn), col3(g_ffn)
    wd_bf16 = cast_bf16(ffn_wd, "cast_wd")

    xf = x.reshape(m, d)
    memf = mem.reshape(b * n_mem, d)
    xb, xr = row_stats(xf, "x_stats")

    for l in range(depth):
        last = l == depth - 1
        tm, tn = TILE_IN_PROJ
        z = matmul([xb], w_in, l, F32, f"in_proj{l}", tn=tn, tm=tm, norm_in=(g_mix_c, xr),
                   w_chunks=W_PREFETCH_CHUNKS)
        z3 = z.reshape(b, seq, z.shape[1])
        y_attn = attention(z, g_q3, g_k3, cos_t, sin_t, l, b, seq, f"attn{l}")
        y_rec = rglru(z3, rg_conv_w, rg_cb3, rg_w4, rg_b_a, rg_b_x, rg_lam, l, f"rglru{l}")
        y_cv_n = conformer(z3, cv_dw_w, cv_db3, cv_lg3, cv_lb3, cv_pw_w, cv_pb3, g_grp3, l, f"conformer{l}")
        y_attn_n = rmsnorm(y_attn, g_grp3, l, 0, BF16, f"norm_attn{l}")
        y_rec_n = rmsnorm(y_rec.reshape(m, RG_WIDTH), g_grp3, l, ATTN_WIDTH // RG_WIDTH, BF16, f"norm_rec{l}")
        tm, tn = TILE_OUT_PROJ
        xf, xb, xr = matmul([y_attn_n, y_rec_n, y_cv_n.reshape(m, CV_WIDTH)], w_out, l, F32, f"out_proj{l}",
                            tn=tn, tm=tm, res=xf, emit_norm=True, single_buffer_w=True)

        mn = rmsnorm(memf, g_mem3, l, 0, BF16, f"norm_mem{l}")
        tm, tn = TILE_XA_KV
        k2 = matmul([mn], xa_wk, l, BF16, f"xa_k{l}", tn=tn, tm=tm)
        v2 = matmul([mn], xa_wv, l, BF16, f"xa_v{l}", tn=tn, tm=tm)
        o2 = q_proj_cross_attention(xb, xa_wq, g_xattn_c, xr, k2, v2, l, seq, n_mem, f"xa_q_attn{l}",
                                    tm=TILE_XA_Q, w_chunks=W_PREFETCH_CHUNKS)
        tm, tn = TILE_XA_O
        xf, xb, xr = matmul([o2], xa_wo, l, F32, f"xa_o{l}", tn=tn, tm=tm, res=xf, emit_norm=True,
                            single_buffer_w=True)

        tm, tn = TILE_FFN_UP
        act = swiglu(xb, ffn_wg, ffn_wu, g_ffn_c, xr, l, f"ffn_up{l}", tn=tn, tm=tm,
                     w_chunks=W_PREFETCH_CHUNKS)
        tm, tn = TILE_FFN_DOWN
        if last:
            xf = matmul([act], wd_bf16, l, F32, f"ffn_down{l}", tn=tn, tm=tm, res=xf)
        else:
            xf, xb, xr = matmul([act], wd_bf16, l, F32, f"ffn_down{l}", tn=tn, tm=tm, res=xf, emit_norm=True)

    out = rmsnorm(xf, g_final3, 0, 0, F32, "norm_final")
    return out.reshape(b, seq, d)
```

```python
import functools
import math

import jax
import jax.numpy as jnp
from jax import lax
from jax.experimental import pallas as pl
from jax.experimental.pallas import tpu as pltpu

F32 = jnp.float32
BF16 = jnp.bfloat16

HEAD_DIM = 128
N_Q_HEADS = 16
N_KV_HEADS = 4
Q_PER_KV = N_Q_HEADS // N_KV_HEADS
ATTN_WIDTH = N_Q_HEADS * HEAD_DIM
KV_WIDTH = N_KV_HEADS * HEAD_DIM
RG_WIDTH = 1024
RG_BLOCK = 128
RG_CONV_W = 4
RG_C = 8.0
CV_WIDTH = 1024
CV_KERNEL = 31
QKV_WIDTH = ATTN_WIDTH + 2 * KV_WIDTH
COL_RX = QKV_WIDTH // RG_WIDTH
COL_RG = COL_RX + 1
COL_CA = COL_RG + 1
COL_CB = COL_CA + 1
MEM_HEADS = 4
GRID_W = 64
ROPE_THETA = 10000.0
ROPE_AXIS_DIM = HEAD_DIM // 2
EPS = 1e-6

V7X_VMEM_BYTES = 64 * 1024 * 1024
V7X_SUBLANES = 8
V7X_LANES = 128
VMEM_LIMIT = V7X_VMEM_BYTES - 4 * 1024 * 1024

CV_HALO = 16


def _sigmoid(x):
    return 0.5 * jnp.tanh(0.5 * x) + 0.5


def _cparams(sem):
    return pltpu.CompilerParams(dimension_semantics=sem, vmem_limit_bytes=VMEM_LIMIT)


def _rmsnorm_kernel(x_ref, g_ref, o_ref):
    x = x_ref[...].astype(F32)
    ms = jnp.mean(x * x, axis=-1, keepdims=True)
    o_ref[...] = (x * lax.rsqrt(ms + EPS) * g_ref[...]).astype(o_ref.dtype)


def rmsnorm(x, g3, layer, gcol, out_dtype, name, tm=512):
    m, w = x.shape
    tm = min(tm, m)
    return pl.pallas_call(
        _rmsnorm_kernel,
        out_shape=jax.ShapeDtypeStruct((m, w), out_dtype),
        grid=(m // tm,),
        in_specs=[
            pl.BlockSpec((tm, w), lambda i: (i, 0)),
            pl.BlockSpec((None, 1, w), lambda i: (layer, 0, gcol)),
        ],
        out_specs=pl.BlockSpec((tm, w), lambda i: (i, 0)),
        compiler_params=_cparams(("arbitrary",)),
        name=name,
    )(x, g3)


CAST_ROWS = 256
W_PREFETCH_CHUNKS = 4

TILE_IN_PROJ = (512, 1024)
TILE_OUT_PROJ = (512, 1024)
TILE_XA_KV = (1024, 512)
TILE_XA_Q = 512
TILE_XA_O = (512, 1024)
TILE_FFN_UP = (1024, 256)
TILE_FFN_DOWN = (512, 512)


def _cast_panel(w_refs, wb_ref, g_ref=None):
    kc, tn = w_refs[0].shape

    for ci, w_ref in enumerate(w_refs):
        def body(i, c, w_ref=w_ref, base=ci * kc):
            src = pl.ds(pl.multiple_of(i * CAST_ROWS, CAST_ROWS), CAST_ROWS)
            dst = pl.ds(pl.multiple_of(base + i * CAST_ROWS, CAST_ROWS), CAST_ROWS)
            if g_ref is None:
                wb_ref[dst, :] = w_ref[src, :].astype(BF16)
            else:
                g = g_ref[dst, :]
                for c0 in range(0, tn, V7X_LANES):
                    wb_ref[dst, c0:c0 + V7X_LANES] = (w_ref[src, c0:c0 + V7X_LANES] * g).astype(BF16)
            return c

        lax.fori_loop(0, kc // CAST_ROWS, body, 0)


def _n_w_chunks(requested, n_row_tiles):
    limit = max(1, min(requested, n_row_tiles - 1))
    return 1 << (limit.bit_length() - 1)


def _w_chunk_specs(k, tn, layer, n_panels, n_chunks):
    kc = k // n_chunks
    assert kc * n_chunks == k and kc % CAST_ROWS == 0

    def spec(c):
        def index(j, i):
            return layer, c, jnp.where(i <= c, j, jnp.minimum(j + 1, n_panels - 1))
        return pl.BlockSpec((None, kc, tn), index)

    return [spec(c) for c in range(n_chunks)]


STAT_LANES = V7X_LANES


def _mm_kernel(*refs, k_splits, n_w, has_norm_in, has_res, emit_norm, w_is_bf16, n_total):
    n_a = len(k_splits)
    a_refs = refs[:n_a]
    w_refs = refs[n_a:n_a + n_w]
    pos = n_a + n_w
    g_ref = rin_ref = res_ref = xb_ref = rout_ref = ssq_ref = None
    if has_norm_in:
        g_ref, rin_ref = refs[pos], refs[pos + 1]
        pos += 2
    if has_res:
        res_ref = refs[pos]
        pos += 1
    o_ref = refs[pos]
    pos += 1
    if emit_norm:
        xb_ref, rout_ref = refs[pos], refs[pos + 1]
        pos += 2
    j = pl.program_id(0)
    i = pl.program_id(1)
    if w_is_bf16:
        wb_ref = w_refs[0]
    else:
        wb_ref = refs[pos]
        pos += 1

        @pl.when(i == 0)
        def _():
            _cast_panel(w_refs, wb_ref, g_ref)

    acc = None
    off = 0
    for a_ref, kk in zip(a_refs, k_splits):
        d = jnp.dot(a_ref[...], wb_ref[off:off + kk, :], preferred_element_type=F32)
        acc = d if acc is None else acc + d
        off += kk
    if has_norm_in:
        acc = acc * rin_ref[:, 0:1]
    if has_res:
        acc = acc + res_ref[...]
    o_ref[...] = acc.astype(o_ref.dtype)
    if emit_norm:
        ssq_ref = refs[pos]
        tm = acc.shape[0]
        xb_ref[...] = acc.astype(BF16)
        rows = pl.ds(pl.multiple_of(i * tm, tm), tm)
        part = jnp.broadcast_to(jnp.sum(acc * acc, axis=-1, keepdims=True), (tm, STAT_LANES))

        @pl.when(j == 0)
        def _():
            ssq_ref[rows, :] = part

        @pl.when(j > 0)
        def _():
            ssq_ref[rows, :] = ssq_ref[rows, :] + part

        rout_ref[...] = lax.rsqrt(ssq_ref[rows, :] * (1.0 / n_total) + EPS)


def matmul(a_list, w3, layer, out_dtype, name, *, tn, tm, norm_in=None, res=None, emit_norm=False,
           single_buffer_w=False, w_chunks=1):
    m = a_list[0].shape[0]
    k_splits = tuple(a.shape[1] for a in a_list)
    _, k, n = w3.shape
    assert sum(k_splits) == k
    tm = min(tm, m)
    w_is_bf16 = w3.dtype == BF16
    assert not (w_is_bf16 and norm_in is not None)
    in_specs = [pl.BlockSpec((tm, kk), lambda j, i: (i, 0)) for kk in k_splits]
    assert w_chunks == 1 or not (w_is_bf16 or single_buffer_w)
    w_chunks = _n_w_chunks(w_chunks, m // tm)
    if w_chunks > 1:
        in_specs += _w_chunk_specs(k, tn, layer, n // tn, w_chunks)
    else:
        w_mode = dict(pipeline_mode=pl.Buffered(1)) if single_buffer_w else {}
        in_specs.append(pl.BlockSpec((None, k, tn), lambda j, i: (layer, 0, j), **w_mode))
    args = list(a_list) + [w3] * w_chunks
    if norm_in is not None:
        last_slab = norm_in[1].shape[0] - 1
        in_specs.append(pl.BlockSpec((None, k, V7X_LANES), lambda j, i: (layer, 0, 0)))
        in_specs.append(pl.BlockSpec((None, tm, STAT_LANES), lambda j, i: (last_slab, i, 0)))
        args += list(norm_in)
    if res is not None:
        in_specs.append(pl.BlockSpec((tm, tn), lambda j, i: (i, j)))
        args.append(res)
    out_shape = [jax.ShapeDtypeStruct((m, n), out_dtype)]
    out_specs = [pl.BlockSpec((tm, tn), lambda j, i: (i, j))]
    scratch = [] if w_is_bf16 else [pltpu.VMEM((k, tn), BF16)]
    if emit_norm:
        out_shape += [jax.ShapeDtypeStruct((m, n), BF16), jax.ShapeDtypeStruct((n // tn, m, STAT_LANES), F32)]
        out_specs += [pl.BlockSpec((tm, tn), lambda j, i: (i, j)),
                      pl.BlockSpec((None, tm, STAT_LANES), lambda j, i: (j, i, 0))]
        scratch.append(pltpu.VMEM((m, STAT_LANES), F32))
    kern = functools.partial(_mm_kernel, k_splits=k_splits, n_w=w_chunks, has_norm_in=norm_in is not None,
                             has_res=res is not None, emit_norm=emit_norm, w_is_bf16=w_is_bf16, n_total=n)
    out = pl.pallas_call(
        kern,
        out_shape=tuple(out_shape),
        grid=(n // tn, m // tm),
        in_specs=in_specs,
        out_specs=tuple(out_specs),
        scratch_shapes=scratch,
        compiler_params=_cparams(("arbitrary", "arbitrary")),
        name=name,
    )(*args)
    return out if emit_norm else out[0]


def _row_stats_kernel(x_ref, xb_ref, r_ref):
    x = x_ref[...]
    xb_ref[...] = x.astype(BF16)
    ms = jnp.mean(x * x, axis=-1, keepdims=True)
    r_ref[...] = jnp.broadcast_to(lax.rsqrt(ms + EPS), r_ref.shape)


def row_stats(x, name, tm=512):
    m, w = x.shape
    tm = min(tm, m)
    return pl.pallas_call(
        _row_stats_kernel,
        out_shape=(jax.ShapeDtypeStruct((m, w), BF16), jax.ShapeDtypeStruct((1, m, STAT_LANES), F32)),
        grid=(m // tm,),
        in_specs=[pl.BlockSpec((tm, w), lambda i: (i, 0))],
        out_specs=(pl.BlockSpec((tm, w), lambda i: (i, 0)),
                   pl.BlockSpec((None, tm, STAT_LANES), lambda i: (0, i, 0))),
        compiler_params=_cparams(("arbitrary",)),
        name=name,
    )(x)


def _swiglu_kernel(*refs, n_w):
    a_ref = refs[0]
    wg_refs = refs[1:1 + n_w]
    wu_refs = refs[1 + n_w:1 + 2 * n_w]
    g_ref, rin_ref, o_ref, wgb_ref, wub_ref = refs[1 + 2 * n_w:]

    @pl.when(pl.program_id(1) == 0)
    def _():
        _cast_panel(wg_refs, wgb_ref, g_ref)
        _cast_panel(wu_refs, wub_ref, g_ref)

    a = a_ref[...]
    r = rin_ref[:, 0:1]
    g = jnp.dot(a, wgb_ref[...], preferred_element_type=F32) * r
    u = jnp.dot(a, wub_ref[...], preferred_element_type=F32) * r
    o_ref[...] = (g * _sigmoid(g) * u).astype(o_ref.dtype)


def swiglu(a, wg3, wu3, g3, rin, layer, name, *, tn, tm, w_chunks):
    m, k = a.shape
    n = wg3.shape[2]
    tm = min(tm, m)
    assert n % tn == 0
    w_chunks = _n_w_chunks(w_chunks, m // tm)
    wspecs = _w_chunk_specs(k, tn, layer, n // tn, w_chunks)
    last_slab = rin.shape[0] - 1
    return pl.pallas_call(
        functools.partial(_swiglu_kernel, n_w=w_chunks),
        out_shape=jax.ShapeDtypeStruct((m, n), BF16),
        grid=(n // tn, m // tm),
        in_specs=[pl.BlockSpec((tm, k), lambda j, i: (i, 0))] + wspecs + wspecs + [
            pl.BlockSpec((None, k, V7X_LANES), lambda j, i: (layer, 0, 0)),
            pl.BlockSpec((None, tm, STAT_LANES), lambda j, i: (last_slab, i, 0))],
        out_specs=pl.BlockSpec((tm, tn), lambda j, i: (i, j)),
        scratch_shapes=[pltpu.VMEM((k, tn), BF16), pltpu.VMEM((k, tn), BF16)],
        compiler_params=_cparams(("arbitrary", "arbitrary")),
        name=name,
    )(a, *([wg3] * w_chunks), *([wu3] * w_chunks), g3, rin)


def _cast_kernel(x_ref, o_ref):
    o_ref[...] = x_ref[...].astype(o_ref.dtype)


def cast_bf16(w3, name, rows=256):
    l, k, n = w3.shape
    return pl.pallas_call(
        _cast_kernel,
        out_shape=jax.ShapeDtypeStruct((l, k, n), BF16),
        grid=(l, k // rows),
        in_specs=[pl.BlockSpec((None, rows, n), lambda a, i: (a, i, 0))],
        out_specs=pl.BlockSpec((None, rows, n), lambda a, i: (a, i, 0)),
        compiler_params=_cparams(("arbitrary", "arbitrary")),
        name=name,
    )(w3)


def rope_tables(seq):
    rows = seq // GRID_W
    row = jnp.repeat(jnp.arange(rows), GRID_W).astype(F32)
    col = jnp.tile(jnp.arange(GRID_W), rows).astype(F32)
    inv = ROPE_THETA ** (-jnp.arange(0, ROPE_AXIS_DIM, 2, dtype=F32) / ROPE_AXIS_DIM)
    ang_r = row[:, None] * inv
    ang_c = col[:, None] * inv
    cos_t = jnp.concatenate([jnp.cos(ang_r), jnp.cos(ang_r), jnp.cos(ang_c), jnp.cos(ang_c)], axis=1)
    sin_t = jnp.concatenate([-jnp.sin(ang_r), jnp.sin(ang_r), -jnp.sin(ang_c), jnp.sin(ang_c)], axis=1)
    return cos_t, sin_t


def _norm_rope(x, g, cos, sin):
    half = ROPE_AXIS_DIM // 2
    lane = lax.broadcasted_iota(jnp.int32, (1, HEAD_DIM), 1)
    first_half = (lane % ROPE_AXIS_DIM) < half
    ms = jnp.mean(x * x, axis=-1, keepdims=True)
    xn = x * lax.rsqrt(ms + EPS) * g
    swapped = jnp.where(first_half, pltpu.roll(xn, HEAD_DIM - half, axis=1), pltpu.roll(xn, half, axis=1))
    return xn * cos + swapped * sin


def _attn_kernel(zq_ref, zk_ref, zv_ref, gq_ref, gk_ref, cos_ref, sin_ref, o_ref, k_s, v_s, *, tq, seq):
    i = pl.program_id(2)

    @pl.when(i == 0)
    def _():
        k_s[...] = _norm_rope(zk_ref[...], gk_ref[...], cos_ref[...], sin_ref[...]).astype(BF16)
        v_s[:, 0:HEAD_DIM] = zv_ref[...].astype(BF16)
        v_s[:, HEAD_DIM:2 * HEAD_DIM] = jnp.ones((seq, HEAD_DIM), BF16)

    r0 = pl.multiple_of(i * tq, tq)
    cos = cos_ref[pl.ds(r0, tq), :]
    sin = sin_ref[pl.ds(r0, tq), :]
    gq = gq_ref[...]
    k = k_s[...]
    v = v_s[...]
    c = (HEAD_DIM ** -0.5) * math.log2(math.e)
    for h in range(Q_PER_KV):
        c0 = h * HEAD_DIM
        q = _norm_rope(zq_ref[:, c0:c0 + HEAD_DIM], gq, cos, sin).astype(BF16)
        s = lax.dot_general(q, k, (((1,), (1,)), ((), ())), preferred_element_type=F32)
        mx = jnp.max(s, axis=-1, keepdims=True)
        p = jnp.exp2((s - mx) * c).astype(BF16)
        o = jnp.dot(p, v, preferred_element_type=F32)
        o_ref[:, c0:c0 + HEAD_DIM] = (o[:, 0:HEAD_DIM] / o[:, HEAD_DIM:2 * HEAD_DIM]).astype(o_ref.dtype)


def attention(z, gq3, gk3, cos_t, sin_t, layer, batch, seq, name, tq=256):
    m = z.shape[0]
    tq = min(tq, seq)
    nq = seq // tq
    gw = Q_PER_KV * HEAD_DIM
    kcol = ATTN_WIDTH // HEAD_DIM
    vcol = (ATTN_WIDTH + KV_WIDTH) // HEAD_DIM
    gspec = pl.BlockSpec((None, 1, HEAD_DIM), lambda b, g, i: (layer, 0, 0))
    tspec = pl.BlockSpec((seq, HEAD_DIM), lambda b, g, i: (0, 0))
    kern = functools.partial(_attn_kernel, tq=tq, seq=seq)
    return pl.pallas_call(
        kern,
        out_shape=jax.ShapeDtypeStruct((m, ATTN_WIDTH), F32),
        grid=(batch, N_KV_HEADS, nq),
        in_specs=[
            pl.BlockSpec((tq, gw), lambda b, g, i: (b * nq + i, g)),
            pl.BlockSpec((seq, HEAD_DIM), lambda b, g, i: (b, kcol + g)),
            pl.BlockSpec((seq, HEAD_DIM), lambda b, g, i: (b, vcol + g)),
            gspec, gspec, tspec, tspec,
        ],
        out_specs=pl.BlockSpec((tq, gw), lambda b, g, i: (b * nq + i, g)),
        scratch_shapes=[pltpu.VMEM((seq, HEAD_DIM), BF16), pltpu.VMEM((seq, 2 * HEAD_DIM), BF16)],
        compiler_params=_cparams(("arbitrary", "arbitrary", "arbitrary")),
        name=name,
    )(z, z, z, gq3, gk3, cos_t, sin_t)


def _xq_attn_kernel(*refs, n_w):
    x_ref = refs[0]
    w_refs = refs[1:1 + n_w]
    g_ref, rin_ref, k_ref, v_ref, o_ref, wb_ref = refs[1 + n_w:]

    @pl.when(pl.program_id(1) == 0)
    def _():
        _cast_panel(w_refs, wb_ref, g_ref)

    head_dim = wb_ref.shape[1]
    q = jnp.dot(x_ref[...], wb_ref[...], preferred_element_type=F32) * rin_ref[:, 0:1]
    k = k_ref[...]
    v = v_ref[...]
    s = lax.dot_general(q.astype(BF16), k, (((1,), (1,)), ((), ())), preferred_element_type=F32)
    s = s * (head_dim ** -0.5)
    mx = jnp.max(s, axis=-1, keepdims=True)
    p = jnp.exp(s - mx)
    den = jnp.sum(p, axis=-1, keepdims=True)
    o = jnp.dot(p.astype(BF16), v, preferred_element_type=F32)
    o_ref[...] = (o / den).astype(o_ref.dtype)


def q_proj_cross_attention(xb, wq3, g3, rin, k, v, layer, seq, n_mem, name, tm=512, w_chunks=1):
    m, d = xb.shape
    head_dim = d // MEM_HEADS
    tm = min(tm, seq)
    tiles_per_batch = seq // tm
    last_slab = rin.shape[0] - 1
    w_chunks = _n_w_chunks(w_chunks, m // tm)
    kvspec = pl.BlockSpec((n_mem, head_dim), lambda j, i: (i // tiles_per_batch, j))
    return pl.pallas_call(
        functools.partial(_xq_attn_kernel, n_w=w_chunks),
        out_shape=jax.ShapeDtypeStruct((m, d), BF16),
        grid=(MEM_HEADS, m // tm),
        in_specs=[pl.BlockSpec((tm, d), lambda j, i: (i, 0))]
        + _w_chunk_specs(d, head_dim, layer, MEM_HEADS, w_chunks) + [
            pl.BlockSpec((None, d, V7X_LANES), lambda j, i: (layer, 0, 0)),
            pl.BlockSpec((None, tm, STAT_LANES), lambda j, i: (last_slab, i, 0)),
            kvspec, kvspec,
        ],
        out_specs=pl.BlockSpec((tm, head_dim), lambda j, i: (i, j)),
        scratch_shapes=[pltpu.VMEM((d, head_dim), BF16)],
        compiler_params=_cparams(("arbitrary", "arbitrary")),
        name=name,
    )(xb, *([wq3] * w_chunks), g3, rin, k, v)


RG_PAD = V7X_SUBLANES


def _rglru_kernel(x_ref, gate_ref, cw_ref, cb_ref, w_ref, ba_ref, bx_ref, lam_ref, y_ref,
                  pad_ref, af_ref, uf_ref, ab_ref, ub_ref, *, seq, heads):
    cw = heads * RG_BLOCK
    zeros = jnp.zeros((RG_PAD, cw), F32)
    pad_ref[0:RG_PAD, :] = zeros
    pad_ref[RG_PAD + seq:RG_PAD + seq + RG_PAD, :] = zeros
    pad_ref[RG_PAD:RG_PAD + seq, :] = x_ref[...]
    left = RG_CONV_W // 2
    xc = cb_ref[...] + cw_ref[0:1, :] * pad_ref[pl.ds(RG_PAD - left, seq), :]
    for j in range(1, RG_CONV_W):
        xc = xc + cw_ref[j:j + 1, :] * pad_ref[pl.ds(RG_PAD - left + j, seq), :]
    xcb = xc.astype(BF16)

    lam = lam_ref[...]
    nl = -lam
    softplus = jnp.maximum(nl, 0.0) + jnp.log1p(jnp.exp(-jnp.abs(nl)))
    ba = ba_ref[...]
    bx = bx_ref[...]
    for h in range(heads):
        c0 = h * RG_BLOCK
        sl = slice(c0, c0 + RG_BLOCK)
        gates = jnp.dot(xcb[:, sl], w_ref[h].astype(BF16), preferred_element_type=F32)
        xch = xc[:, sl]
        for d, (a_ref, u_ref) in enumerate(((af_ref, uf_ref), (ab_ref, ub_ref))):
            g0 = 2 * d * RG_BLOCK
            r = _sigmoid(gates[:, g0:g0 + RG_BLOCK] + ba[d:d + 1, sl])
            i = _sigmoid(gates[:, g0 + RG_BLOCK:g0 + 2 * RG_BLOCK] + bx[d:d + 1, sl])
            log_a = (-RG_C) * r * softplus[d:d + 1, sl]
            t = jnp.tanh(log_a)
            one_minus_a2 = (-2.0 * t) / (1.0 - t)
            a_ref[:, sl] = jnp.exp(log_a)
            u_ref[:, sl] = jnp.sqrt(one_minus_a2) * (i * xch)

    n_tiles = seq // V7X_SUBLANES
    row = lax.broadcasted_iota(jnp.int32, (V7X_SUBLANES, cw), 0)

    def tile_scan(a, u, reverse):
        for d in (1, 2, 4):
            if reverse:
                shift, keep = V7X_SUBLANES - d, row < V7X_SUBLANES - d
            else:
                shift, keep = d, row >= d
            a_s = pltpu.roll(a, shift, axis=0)
            u_s = pltpu.roll(u, shift, axis=0)
            u = jnp.where(keep, u + a * u_s, u)
            a = jnp.where(keep, a * a_s, a)
        return a, u

    def body(t, carry):
        hf, hb = carry
        rf = pl.multiple_of(t * V7X_SUBLANES, V7X_SUBLANES)
        a, u = tile_scan(af_ref[pl.ds(rf, V7X_SUBLANES), :], uf_ref[pl.ds(rf, V7X_SUBLANES), :], False)
        hh = a * hf + u
        uf_ref[pl.ds(rf, V7X_SUBLANES), :] = hh
        hf = jnp.broadcast_to(hh[V7X_SUBLANES - 1:V7X_SUBLANES, :], hh.shape)
        rb = pl.multiple_of((n_tiles - 1 - t) * V7X_SUBLANES, V7X_SUBLANES)
        a, u = tile_scan(ab_ref[pl.ds(rb, V7X_SUBLANES), :], ub_ref[pl.ds(rb, V7X_SUBLANES), :], True)
        hh = a * hb + u
        ub_ref[pl.ds(rb, V7X_SUBLANES), :] = hh
        hb = jnp.broadcast_to(hh[0:1, :], hh.shape)
        return hf, hb

    h0 = jnp.zeros((V7X_SUBLANES, cw), F32)
    lax.fori_loop(0, n_tiles, body, (h0, h0))

    g = gate_ref[...]
    gelu = 0.5 * g * (1.0 + jnp.tanh(math.sqrt(2.0 / math.pi) * (g + 0.044715 * (g * g * g))))
    y_ref[...] = ((uf_ref[...] + ub_ref[...]) * gelu).astype(y_ref.dtype)


def rglru(z3, cw3, cb3, w4, ba3, bx3, lam3, layer, name, heads=2):
    b, seq, _ = z3.shape
    cw = heads * RG_BLOCK
    ncb = RG_WIDTH // cw
    n_rg_heads = RG_WIDTH // RG_BLOCK
    kern = functools.partial(_rglru_kernel, seq=seq, heads=heads)
    vec = lambda rows: pl.BlockSpec((None, rows, cw), lambda bi, j: (layer, 0, j))
    return pl.pallas_call(
        kern,
        out_shape=jax.ShapeDtypeStruct((b, seq, RG_WIDTH), F32),
        grid=(b, ncb),
        in_specs=[
            pl.BlockSpec((None, seq, cw), lambda bi, j: (bi, 0, COL_RX * ncb + j)),
            pl.BlockSpec((None, seq, cw), lambda bi, j: (bi, 0, COL_RG * ncb + j)),
            vec(RG_CONV_W), vec(1),
            pl.BlockSpec((heads, RG_BLOCK, 4 * RG_BLOCK), lambda bi, j: ((layer * n_rg_heads) // heads + j, 0, 0)),
            vec(2), vec(2), vec(2),
        ],
        out_specs=pl.BlockSpec((None, seq, cw), lambda bi, j: (bi, 0, j)),
        scratch_shapes=[pltpu.VMEM((seq + 2 * RG_PAD, cw), F32)] + [pltpu.VMEM((seq, cw), F32)] * 4,
        compiler_params=_cparams(("arbitrary", "arbitrary")),
        name=name,
    )(z3, z3, cw3, cb3, w4, ba3, bx3, lam3)


CV_CHUNK = 32


def _conformer_kernel(a_ref, g_ref, ap_ref, gp_ref, an_ref, gn_ref, dw_ref, db_ref, lg_ref, lb_ref,
                      pw_ref, pb_ref, gg_ref, o_ref, buf_ref, sh_ref, conv_ref, pwb_ref, dwb_ref, *, ts):
    i = pl.program_id(1)
    n_i = pl.num_programs(1)

    @pl.when(jnp.logical_and(pl.program_id(0) == 0, i == 0))
    def _():
        pwb_ref[...] = pw_ref[...].astype(BF16)
        for j in range(CV_KERNEL):
            dwb_ref[j] = jnp.broadcast_to(dw_ref[j:j + 1, :], (V7X_SUBLANES, CV_WIDTH))

    def glu(a, g):
        return a * _sigmoid(g)

    prev = glu(ap_ref[...], gp_ref[...])
    nxt = glu(an_ref[...], gn_ref[...])
    buf_ref[0:CV_HALO, :] = jnp.where(i > 0, prev, 0.0)
    buf_ref[CV_HALO:CV_HALO + ts, :] = glu(a_ref[...], g_ref[...])
    buf_ref[CV_HALO + ts:CV_HALO + ts + CV_HALO, :] = jnp.where(i < n_i - 1, nxt, 0.0)

    span = ts + 2 * CV_HALO - V7X_SUBLANES
    for r in range(V7X_SUBLANES):
        sh_ref[r] = buf_ref[pl.ds(r, span), :]

    base_off = CV_HALO - CV_KERNEL // 2
    n_sub = CV_CHUNK // V7X_SUBLANES
    bias = jnp.broadcast_to(db_ref[...], (V7X_SUBLANES, CV_WIDTH))

    def chunk(c, carry):
        r0 = pl.multiple_of(c * CV_CHUNK, CV_CHUNK)
        accs = [bias] * n_sub
        for j in range(CV_KERNEL):
            off = base_off + j
            q, r = off // V7X_SUBLANES, off % V7X_SUBLANES
            w = dwb_ref[j]
            for s in range(n_sub):
                rows = pl.multiple_of(r0 + (q + s) * V7X_SUBLANES, V7X_SUBLANES)
                accs[s] = accs[s] + w * sh_ref[r, pl.ds(rows, V7X_SUBLANES), :]
        for s in range(n_sub):
            rows = pl.multiple_of(r0 + s * V7X_SUBLANES, V7X_SUBLANES)
            conv_ref[pl.ds(rows, V7X_SUBLANES), :] = accs[s]
        return carry

    lax.fori_loop(0, ts // CV_CHUNK, chunk, 0)

    y = conv_ref[...]
    mu = jnp.mean(y, axis=-1, keepdims=True)
    yc = y - mu
    var = jnp.mean(yc * yc, axis=-1, keepdims=True)
    yn = yc * lax.rsqrt(var + EPS) * lg_ref[...] + lb_ref[...]
    act = yn * _sigmoid(yn)
    out = jnp.dot(act.astype(BF16), pwb_ref[...], preferred_element_type=F32) + pb_ref[...]
    ms = jnp.mean(out * out, axis=-1, keepdims=True)
    o_ref[...] = (out * lax.rsqrt(ms + EPS) * gg_ref[...]).astype(o_ref.dtype)


def conformer(z3, dw3, db3, lg3, lb3, pw3, pb3, gg3, layer, name, ts=512):
    b, seq, _ = z3.shape
    ts = min(ts, seq)
    nt = seq // ts
    hb = ts // CV_HALO
    n_hblk = seq // CV_HALO
    main = lambda col: pl.BlockSpec((None, ts, CV_WIDTH), lambda bi, i: (bi, i, col))
    prev = lambda col: pl.BlockSpec((None, CV_HALO, CV_WIDTH),
                                    lambda bi, i: (bi, jnp.maximum(i * hb - 1, 0), col))
    nxt = lambda col: pl.BlockSpec((None, CV_HALO, CV_WIDTH),
                                   lambda bi, i: (bi, jnp.minimum((i + 1) * hb, n_hblk - 1), col))
    vec = lambda rows: pl.BlockSpec((None, rows, CV_WIDTH), lambda bi, i: (layer, 0, 0))
    kern = functools.partial(_conformer_kernel, ts=ts)
    return pl.pallas_call(
        kern,
        out_shape=jax.ShapeDtypeStruct((b, seq, CV_WIDTH), BF16),
        grid=(b, nt),
        in_specs=[
            main(COL_CA), main(COL_CB), prev(COL_CA), prev(COL_CB), nxt(COL_CA), nxt(COL_CB),
            vec(CV_KERNEL), vec(1), vec(1), vec(1),
            pl.BlockSpec((None, CV_WIDTH, CV_WIDTH), lambda bi, i: (layer, 0, 0)),
            vec(1),
            pl.BlockSpec((None, 1, CV_WIDTH), lambda bi, i: (layer, 0, (ATTN_WIDTH + RG_WIDTH) // CV_WIDTH)),
        ],
        out_specs=pl.BlockSpec((None, ts, CV_WIDTH), lambda bi, i: (bi, i, 0)),
        scratch_shapes=[
            pltpu.VMEM((ts + 2 * CV_HALO, CV_WIDTH), F32),
            pltpu.VMEM((V7X_SUBLANES, ts + 2 * CV_HALO - V7X_SUBLANES, CV_WIDTH), F32),
            pltpu.VMEM((ts, CV_WIDTH), F32),
            pltpu.VMEM((CV_WIDTH, CV_WIDTH), BF16),
            pltpu.VMEM((CV_KERNEL, V7X_SUBLANES, CV_WIDTH), F32),
        ],
        compiler_params=_cparams(("arbitrary", "arbitrary")),
        name=name,
    )(z3, z3, z3, z3, z3, z3, dw3, db3, lg3, lb3, pw3, pb3, gg3)


def kernel(x, mem, g_mix, w_in, g_q, g_k, rg_conv_w, rg_conv_b, rg_w_a, rg_b_a, rg_w_x, rg_b_x, rg_lam, cv_dw_w, cv_dw_b, cv_ln_g, cv_ln_b, cv_pw_w, cv_pw_b, g_grp, w_out, g_xattn, g_mem, xa_wq, xa_wk, xa_wv, xa_wo, g_ffn, ffn_wg, ffn_wu, ffn_wd, g_final):
    b, seq, d = x.shape
    n_mem = mem.shape[1]
    depth = w_in.shape[0]
    m = b * seq
    row3 = lambda p: p.reshape(p.shape[0], 1, p.shape[-1])

    cos_t, sin_t = rope_tables(seq)
    g_q3, g_k3, g_grp3, g_mem3 = row3(g_q), row3(g_k), row3(g_grp), row3(g_mem)
    g_final3 = g_final.reshape(1, 1, d)
    rg_cb3, cv_db3, cv_lg3, cv_lb3, cv_pb3 = (row3(rg_conv_b), row3(cv_dw_b), row3(cv_ln_g),
                                              row3(cv_ln_b), row3(cv_pw_b))
    rg_w4 = jnp.concatenate([rg_w_a[:, 0], rg_w_x[:, 0], rg_w_a[:, 1], rg_w_x[:, 1]], axis=-1)
    rg_w4 = rg_w4.reshape(-1, RG_BLOCK, 4 * RG_BLOCK)

    col3 = lambda p: jnp.broadcast_to(p[:, :, None], (p.shape[0], p.shape[1], V7X_LANES))
    g_mix_c, g_xattn_c, g_ffn_c = col3(g_mix), col3(g_xattn), col3(g_ffn)
    wd_bf16 = cast_bf16(ffn_wd, "cast_wd")

    xf = x.reshape(m, d)
    memf = mem.reshape(b * n_mem, d)
    xb, xr = row_stats(xf, "x_stats")

    for l in range(depth):
        last = l == depth - 1
        tm, tn = TILE_IN_PROJ
        z = matmul([xb], w_in, l, F32, f"in_proj{l}", tn=tn, tm=tm, norm_in=(g_mix_c, xr),
                   w_chunks=W_PREFETCH_CHUNKS)
        z3 = z.reshape(b, seq, z.shape[1])
        y_attn = attention(z, g_q3, g_k3, cos_t, sin_t, l, b, seq, f"attn{l}")
        y_rec = rglru(z3, rg_conv_w, rg_cb3, rg_w4, rg_b_a, rg_b_x, rg_lam, l, f"rglru{l}")
        y_cv_n = conformer(z3, cv_dw_w, cv_db3, cv_lg3, cv_lb3, cv_pw_w, cv_pb3, g_grp3, l, f"conformer{l}")
        y_attn_n = rmsnorm(y_attn, g_grp3, l, 0, BF16, f"norm_attn{l}")
        y_rec_n = rmsnorm(y_rec.reshape(m, RG_WIDTH), g_grp3, l, ATTN_WIDTH // RG_WIDTH, BF16, f"norm_rec{l}")
        tm, tn = TILE_OUT_PROJ
        xf, xb, xr = matmul([y_attn_n, y_rec_n, y_cv_n.reshape(m, CV_WIDTH)], w_out, l, F32, f"out_proj{l}",
                            tn=tn, tm=tm, res=xf, emit_norm=True, single_buffer_w=True)

        mn = rmsnorm(memf, g_mem3, l, 0, BF16, f"norm_mem{l}")
        tm, tn = TILE_XA_KV
        k2 = matmul([mn], xa_wk, l, BF16, f"xa_k{l}", tn=tn, tm=tm)
        v2 = matmul([mn], xa_wv, l, BF16, f"xa_v{l}", tn=tn, tm=tm)
        o2 = q_proj_cross_attention(xb, xa_wq, g_xattn_c, xr, k2, v2, l, seq, n_mem, f"xa_q_attn{l}",
                                    tm=TILE_XA_Q, w_chunks=W_PREFETCH_CHUNKS)
        tm, tn = TILE_XA_O
        xf, xb, xr = matmul([o2], xa_wo, l, F32, f"xa_o{l}", tn=tn, tm=tm, res=xf, emit_norm=True,
                            single_buffer_w=True)

        tm, tn = TILE_FFN_UP
        act = swiglu(xb, ffn_wg, ffn_wu, g_ffn_c, xr, l, f"ffn_up{l}", tn=tn, tm=tm,
                     w_chunks=W_PREFETCH_CHUNKS)
        tm, tn = TILE_FFN_DOWN
        if last:
            xf = matmul([act], wd_bf16, l, F32, f"ffn_down{l}", tn=tn, tm=tm, res=xf)
        else:
            xf, xb, xr = matmul([act], wd_bf16, l, F32, f"ffn_down{l}", tn=tn, tm=tm, res=xf, emit_norm=True)

    out = rmsnorm(xf, g_final3, 0, 0, F32, "norm_final")
    return out.reshape(b, seq, d)
```
